```python
import math
import jax
import jax.numpy as jnp
from jax import lax
import numpy as np

D_MODEL = 1024
BATCH = 8
SEQ = 4096
DEPTH = 4

GRID_W = 64
CTX_LEN = 256
N_MIXERS = 4
N_MOD = 6
Q_BLOCK = 128
ROPE_THETA = 10000.0
NORM_EPS = 1e-6

A_HEADS = 8
A_DK = 128
A_DV = D_MODEL // A_HEADS
A_WIDTH = A_HEADS * A_DK
A_CHUNK = 32
B_HEADS = 16
B_Q_RANK = 384
B_KV_RANK = 256
B_NOPE = 64
B_ROPE = 32
B_V = 64
C_HEADS = 8
C_KV_HEADS = 2
C_HEAD_DIM = 128
D_HEADS = 8
D_QK_DIM = 64
D_V_DIM = 2 * D_QK_DIM
FF_DIM = 3584
N_EXPERTS = 8
TOP_K = 2

kernel_name = "hybrid_latent_diffusion_block"


def rms_norm(x, g):
    xf = x.astype(jnp.float32)
    y = xf * lax.rsqrt(jnp.mean(xf * xf, axis=-1, keepdims=True) + NORM_EPS)
    return (y * g.astype(jnp.float32)).astype(x.dtype)


def modulate(x, shift, scale):
    return x * (1 + scale) + shift


def split_last(z, sizes):
    idx, acc = [], 0
    for s in sizes[:-1]:
        acc += s
        idx.append(acc)
    return jnp.split(z, idx, axis=-1)


def _rotate(x, pos):
    n = x.shape[-1]
    inv = ROPE_THETA ** (-jnp.arange(0, n, 2, dtype=jnp.float32) / n)
    ang = pos.astype(jnp.float32)[:, None] * inv[None, :]
    cos, sin = jnp.cos(ang).astype(x.dtype), jnp.sin(ang).astype(x.dtype)
    x1, x2 = x[..., : n // 2], x[..., n // 2:]
    return jnp.concatenate([x1 * cos - x2 * sin, x2 * cos + x1 * sin], axis=-1)


def axial_rope(x, pos_row, pos_col):
    half = x.shape[-1] // 2
    return jnp.concatenate([_rotate(x[..., :half], pos_row), _rotate(x[..., half:], pos_col)], axis=-1)


def attend(q, k, v, scale):
    s = jnp.einsum("bhgqd,bhkd->bhgqk", q, k, preferred_element_type=jnp.float32) * scale
    p = jax.nn.softmax(s, axis=-1).astype(v.dtype)
    return jnp.einsum("bhgqk,bhkv->bhgqv", p, v)


def blocked_attend(q, k, v, scale):
    b, hk, g, lq, dk = q.shape
    nb = lq // Q_BLOCK
    qb = jnp.moveaxis(q.reshape(b, hk, g, nb, Q_BLOCK, dk), 3, 0)
    ob = lax.map(lambda blk: attend(blk, k, v, scale), qb)
    return jnp.moveaxis(ob, 0, 3).reshape(b, hk, g, lq, v.shape[-1])


def merge_heads(o):
    b, hk, g, l, dv = o.shape
    return o.transpose(0, 3, 1, 2, 4).reshape(b, l, hk * g * dv)


def chunked_gla(q, k, v, log_f, s0, with_output):
    bsz, h, l, _ = k.shape
    dv = v.shape[-1]
    n = l // A_CHUNK

    def chunks(t):
        return jnp.moveaxis(t.reshape(bsz, h, n, A_CHUNK, t.shape[-1]), 2, 0)

    k, v, log_f = chunks(k), chunks(v), chunks(log_f)
    cum = jnp.cumsum(log_f, axis=3)
    cum_end = cum[:, :, :, -1:, :]
    k_to_end = k * jnp.exp(cum_end - cum)
    chunk_decay = jnp.exp(cum_end[:, :, :, 0, :])
    if with_output:
        q_dec = chunks(q) * jnp.exp(cum)
        att = jnp.einsum("nbhcd,nbhsd->nbhcs", q_dec, k * jnp.exp(-cum))
        mask = jnp.tril(jnp.ones((A_CHUNK, A_CHUNK), dtype=bool))
        o_intra = jnp.einsum("nbhcs,nbhsv->nbhcv", jnp.where(mask, att, 0.0), v)
        xs = (q_dec, k_to_end, v, chunk_decay)
    else:
        xs = (k_to_end, v, chunk_decay)

    def step(state, inp):
        if with_output:
            qd, kd, vv, dec = inp
            o = jnp.einsum("bhcd,bhdv->bhcv", qd, state)
        else:
            kd, vv, dec = inp
            o = None
        state = dec[..., None] * state + jnp.einsum("bhcd,bhcv->bhdv", kd, vv)
        return state, o

    s_final, o_inter = lax.scan(step, s0, xs)
    if not with_output:
        return None, s_final
    o = jnp.moveaxis(o_intra + o_inter, 0, 2).reshape(bsz, h, l, dv)
    return o, s_final


def hgrn2_mixer(u_lat, u_ctx, w_in, lb_fwd, lb_bwd, onorm_g, w_out, ctx_out):
    def project(u):
        bsz, l, _ = u.shape
        z = (u @ w_in).astype(jnp.float32)
        q, f_fwd, f_bwd, inp, gate = split_last(
            z, (A_WIDTH, A_WIDTH, A_WIDTH, A_HEADS * A_DV, A_HEADS * A_DV))

        def heads(t):
            return t.reshape(bsz, l, A_HEADS, -1).transpose(0, 2, 1, 3)

        dirs = []
        for f_raw, lb in ((f_fwd, lb_fwd), (f_bwd, lb_bwd)):
            f = lb + (1.0 - lb) * jax.nn.sigmoid(f_raw)
            dirs.append((heads(1.0 - f), heads(jnp.log(f))))
        return heads(q) * A_DK ** -0.5, dirs, heads(inp), gate

    def flip(t):
        return jnp.flip(t, axis=2)

    q_c, (fw_c, bw_c), v_c, gate_c = project(u_ctx)
    q_l, (fw_l, bw_l), v_l, gate_l = project(u_lat)
    s0 = jnp.zeros((u_ctx.shape[0], A_HEADS, A_DK, A_DV), jnp.float32)
    o_cf, s_cf = chunked_gla(q_c, fw_c[0], v_c, fw_c[1], s0, ctx_out)
    o_cb, s_cb = chunked_gla(flip(q_c), flip(bw_c[0]), flip(v_c), flip(bw_c[1]), s0, ctx_out)
    o_lf, _ = chunked_gla(q_l, fw_l[0], v_l, fw_l[1], s_cf, True)
    o_lb, _ = chunked_gla(flip(q_l), flip(bw_l[0]), flip(v_l), flip(bw_l[1]), s_cb, True)

    def readout(o, gate, dtype):
        bsz, _, l, _ = o.shape
        o = rms_norm(o, onorm_g).transpose(0, 2, 1, 3).reshape(bsz, l, A_HEADS * A_DV)
        return (o * jax.nn.silu(gate)).astype(dtype) @ w_out

    y_lat = readout(o_lf + flip(o_lb), gate_l, u_lat.dtype)
    y_ctx = readout(o_cf + flip(o_cb), gate_c, u_ctx.dtype) if ctx_out else None
    return y_lat, y_ctx


def mla_mixer(u_lat, u_ctx, w_in, qnorm_g, kvnorm_g, w_qb, w_kvb, w_out, pos_row, pos_col, ctx_out):
    def project(u, rotary):
        bsz, l, _ = u.shape
        c_q, c_kv, k_rope = split_last(u @ w_in, (B_Q_RANK, B_KV_RANK, B_ROPE))
        q = (rms_norm(c_q, qnorm_g) @ w_qb).reshape(bsz, l, B_HEADS, B_NOPE + B_ROPE).transpose(0, 2, 1, 3)
        kv = (rms_norm(c_kv, kvnorm_g) @ w_kvb).reshape(bsz, l, B_HEADS, B_NOPE + B_V).transpose(0, 2, 1, 3)
        q_nope, q_rope = q[..., :B_NOPE], q[..., B_NOPE:]
        k_nope, v = kv[..., :B_NOPE], kv[..., B_NOPE:]
        k_rope = k_rope[:, None]
        if rotary:
            q_rope = axial_rope(q_rope, pos_row, pos_col)
            k_rope = axial_rope(k_rope, pos_row, pos_col)
        q = jnp.concatenate([q_nope, q_rope], axis=-1)[:, :, None]
        k = jnp.concatenate([k_nope, jnp.broadcast_to(k_rope, k_nope.shape[:3] + (B_ROPE,))], axis=-1)
        return q, k, v

    scale = (B_NOPE + B_ROPE) ** -0.5
    q_c, k_c, v_c = project(u_ctx, False)
    q_l, k_l, v_l = project(u_lat, True)
    o_lat = blocked_attend(q_l, jnp.concatenate([k_c, k_l], axis=2), jnp.concatenate([v_c, v_l], axis=2), scale)
    y_lat = merge_heads(o_lat) @ w_out
    y_ctx = merge_heads(attend(q_c, k_c, v_c, scale)) @ w_out if ctx_out else None
    return y_lat, y_ctx


def gqa_mixer(u_lat, u_ctx, w_in, qnorm_g, knorm_g, w_out, pos_row, pos_col, ctx_out):
    group = C_HEADS // C_KV_HEADS

    def project(u, rotary):
        bsz, l, _ = u.shape
        q, k, v = split_last(u @ w_in, (C_HEADS * C_HEAD_DIM, C_KV_HEADS * C_HEAD_DIM, C_KV_HEADS * C_HEAD_DIM))
        q = rms_norm(q.reshape(bsz, l, C_KV_HEADS, group, C_HEAD_DIM), qnorm_g).transpose(0, 2, 3, 1, 4)
        k = rms_norm(k.reshape(bsz, l, C_KV_HEADS, C_HEAD_DIM), knorm_g).transpose(0, 2, 1, 3)
        v = v.reshape(bsz, l, C_KV_HEADS, C_HEAD_DIM).transpose(0, 2, 1, 3)
        if rotary:
            q = axial_rope(q, pos_row, pos_col)
            k = axial_rope(k, pos_row, pos_col)
        return q, k, v

    scale = C_HEAD_DIM ** -0.5
    q_c, k_c, v_c = project(u_ctx, False)
    q_l, k_l, v_l = project(u_lat, True)
    o_lat = blocked_attend(q_l, jnp.concatenate([k_c, k_l], axis=2), jnp.concatenate([v_c, v_l], axis=2), scale)
    y_lat = merge_heads(o_lat) @ w_out
    y_ctx = merge_heads(attend(q_c, k_c, v_c, scale)) @ w_out if ctx_out else None
    return y_lat, y_ctx


def diff_mixer(u_lat, u_ctx, w_in, lam, onorm_g, w_out, lambda_init, pos_row, pos_col, ctx_out):
    def project(u, rotary):
        bsz, l, _ = u.shape
        q, k, v = split_last(u @ w_in, (2 * D_HEADS * D_QK_DIM, 2 * D_HEADS * D_QK_DIM, D_HEADS * D_V_DIM))
        q = q.reshape(bsz, l, D_HEADS, 2, D_QK_DIM).transpose(0, 3, 2, 1, 4)
        k = k.reshape(bsz, l, D_HEADS, 2, D_QK_DIM).transpose(0, 3, 2, 1, 4)
        v = v.reshape(bsz, l, D_HEADS, D_V_DIM).transpose(0, 2, 1, 3)
        if rotary:
            q = axial_rope(q, pos_row, pos_col)
            k = axial_rope(k, pos_row, pos_col)
        return q, k, v

    lam = lam.astype(jnp.float32)
    lam_full = jnp.exp(jnp.sum(lam[0] * lam[1])) - jnp.exp(jnp.sum(lam[2] * lam[3])) + lambda_init
    scale = D_QK_DIM ** -0.5

    def diff_attn(q, k, v, attn):
        a1 = attn(q[:, 0, :, None], k[:, 0], v, scale)
        a2 = attn(q[:, 1, :, None], k[:, 1], v, scale)
        o = a1 - lam_full.astype(a1.dtype) * a2
        return merge_heads(rms_norm(o, onorm_g) * (1.0 - lambda_init)) @ w_out

    q_c, k_c, v_c = project(u_ctx, False)
    q_l, k_l, v_l = project(u_lat, True)
    y_lat = diff_attn(q_l, jnp.concatenate([k_c, k_l], axis=3), jnp.concatenate([v_c, v_l], axis=2), blocked_attend)
    y_ctx = diff_attn(q_c, k_c, v_c, attend) if ctx_out else None
    return y_lat, y_ctx


def swiglu(h, w_gu, w_down):
    gate, up = jnp.split(h @ w_gu, 2, axis=-1)
    return (jax.nn.silu(gate) * up) @ w_down


def moe_swiglu(h, router, w_gu, w_down):
    logits = jnp.einsum("...d,de->...e", h, router, preferred_element_type=jnp.float32)
    top_v, top_i = lax.top_k(logits, TOP_K)
    wts = jax.nn.softmax(top_v, axis=-1)
    combine = jnp.einsum("...k,...ke->...e", wts, jax.nn.one_hot(top_i, N_EXPERTS, dtype=jnp.float32)).astype(h.dtype)
    out = jnp.zeros_like(h)
    for e in range(N_EXPERTS):
        out = out + combine[..., e:e + 1] * swiglu(h, w_gu[e], w_down[e])
    return out


def setup_inputs(seed: int = 0) -> dict:
    key = jax.random.key(seed)
    ks = jax.random.split(key, 30)

    def nrm(k, shape, scale):
        return jax.random.normal(k, shape, jnp.float32) * scale

    d = D_MODEL
    n_a = len(range(0, DEPTH, N_MIXERS))
    n_b = len(range(1, DEPTH, N_MIXERS))
    n_c = len(range(2, DEPTH, N_MIXERS))
    n_d = len(range(3, DEPTH, N_MIXERS))
    n_dense = len(range(0, DEPTH, 2))
    n_moe = len(range(1, DEPTH, 2))
    a_in = 3 * A_WIDTH + 2 * A_HEADS * A_DV
    b_in = B_Q_RANK + B_KV_RANK + B_ROPE
    c_in = (C_HEADS + 2 * C_KV_HEADS) * C_HEAD_DIM
    d_in = 4 * D_HEADS * D_QK_DIM + D_HEADS * D_V_DIM
    return {
        "x": nrm(ks[0], (BATCH, SEQ, d), 1.0),
        "c": nrm(ks[1], (BATCH, d), 1.0),
        "ctx": nrm(ks[2], (BATCH, CTX_LEN, d), 1.0),
        "c_ctx": nrm(ks[3], (d,), 1.0),
        "mod_w": nrm(ks[4], (DEPTH, d, N_MOD * d), 0.2 * d ** -0.5),
        "mod_b": nrm(ks[5], (DEPTH, N_MOD * d), 0.02),
        "norm_g": 1.0 + nrm(ks[6], (DEPTH, 4, d), 0.05),
        "a_w_in": nrm(ks[7], (n_a, d, a_in), d ** -0.5),
        "a_lb": nrm(ks[8], (2, DEPTH + 1, A_WIDTH), 0.1),
        "a_onorm_g": 1.0 + nrm(ks[9], (n_a, A_DV), 0.05),
        "a_w_out": nrm(ks[10], (n_a, A_HEADS * A_DV, d), (A_HEADS * A_DV) ** -0.5),
        "b_w_in": nrm(ks[11], (n_b, d, b_in), d ** -0.5),
        "b_qnorm_g": 1.0 + nrm(ks[12], (n_b, B_Q_RANK), 0.05),
        "b_kvnorm_g": 1.0 + nrm(ks[13], (n_b, B_KV_RANK), 0.05),
        "b_w_qb": nrm(ks[14], (n_b, B_Q_RANK, B_HEADS * (B_NOPE + B_ROPE)), B_Q_RANK ** -0.5),
        "b_w_kvb": nrm(ks[15], (n_b, B_KV_RANK, B_HEADS * (B_NOPE + B_V)), B_KV_RANK ** -0.5),
        "b_w_out": nrm(ks[16], (n_b, B_HEADS * B_V, d), (B_HEADS * B_V) ** -0.5),
        "c_w_in": nrm(ks[17], (n_c, d, c_in), d ** -0.5),
        "c_qnorm_g": 1.0 + nrm(ks[18], (n_c, C_HEAD_DIM), 0.05),
        "c_knorm_g": 1.0 + nrm(ks[19], (n_c, C_HEAD_DIM), 0.05),
        "c_w_out": nrm(ks[20], (n_c, C_HEADS * C_HEAD_DIM, d), (C_HEADS * C_HEAD_DIM) ** -0.5),
        "d_w_in": nrm(ks[21], (n_d, d, d_in), d ** -0.5),
        "d_lambda": nrm(ks[22], (n_d, 4, D_QK_DIM), 0.1),
        "d_onorm_g": 1.0 + nrm(ks[23], (n_d, D_V_DIM), 0.05),
        "d_w_out": nrm(ks[24], (n_d, D_HEADS * D_V_DIM, d), (D_HEADS * D_V_DIM) ** -0.5),
        "ffn_w_gu": nrm(ks[25], (n_dense, d, 2 * FF_DIM), d ** -0.5),
        "ffn_w_down": nrm(ks[26], (n_dense, FF_DIM, d), FF_DIM ** -0.5),
        "moe_router": nrm(ks[27], (n_moe, d, N_EXPERTS), d ** -0.5),
        "moe_w_gu": nrm(ks[28], (n_moe, N_EXPERTS, d, 2 * FF_DIM), d ** -0.5),
        "moe_w_down": nrm(ks[29], (n_moe, N_EXPERTS, FF_DIM, d), FF_DIM ** -0.5),
    }


def reference(x, c, ctx, c_ctx, mod_w, mod_b, norm_g,
              a_w_in, a_lb, a_onorm_g, a_w_out,
              b_w_in, b_qnorm_g, b_kvnorm_g, b_w_qb, b_w_kvb, b_w_out,
              c_w_in, c_qnorm_g, c_knorm_g, c_w_out,
              d_w_in, d_lambda, d_onorm_g, d_w_out,
              ffn_w_gu, ffn_w_down, moe_router, moe_w_gu, moe_w_down):
    seq_len = x.shape[1]
    rows = seq_len // GRID_W
    pos_row = jnp.repeat(jnp.arange(rows, dtype=jnp.int32), GRID_W)
    pos_col = jnp.arange(rows * GRID_W, dtype=jnp.int32) % GRID_W
    cond_lat = jax.nn.silu(c)
    cond_ctx = jax.nn.silu(c_ctx)
    lower_bounds = jnp.cumsum(jax.nn.softmax(a_lb.astype(jnp.float32), axis=1), axis=1)

    h_lat, h_ctx = x, ctx
    for i in range(DEPTH):
        last = i == DEPTH - 1
        kind, j = i % N_MIXERS, i // N_MIXERS
        mod_lat = jnp.split((cond_lat @ mod_w[i] + mod_b[i])[:, None, :], N_MOD, axis=-1)
        mod_ctx = jnp.split(cond_ctx @ mod_w[i] + mod_b[i], N_MOD, axis=-1)

        u_lat = modulate(rms_norm(h_lat, norm_g[i, 0]), mod_lat[0], mod_lat[1])
        u_ctx = modulate(rms_norm(h_ctx, norm_g[i, 0]), mod_ctx[0], mod_ctx[1])
        if kind == 0:
            y_lat, y_ctx = hgrn2_mixer(u_lat, u_ctx, a_w_in[j], lower_bounds[0, i], lower_bounds[1, i],
                                       a_onorm_g[j], a_w_out[j], not last)
        elif kind == 1:
            y_lat, y_ctx = mla_mixer(u_lat, u_ctx, b_w_in[j], b_qnorm_g[j], b_kvnorm_g[j], b_w_qb[j],
                                     b_w_kvb[j], b_w_out[j], pos_row, pos_col, not last)
        elif kind == 2:
            y_lat, y_ctx = gqa_mixer(u_lat, u_ctx, c_w_in[j], c_qnorm_g[j], c_knorm_g[j], c_w_out[j],
                                     pos_row, pos_col, not last)
        else:
            lambda_init = 0.8 - 0.6 * math.exp(-0.3 * i)
            y_lat, y_ctx = diff_mixer(u_lat, u_ctx, d_w_in[j], d_lambda[j], d_onorm_g[j], d_w_out[j],
                                      lambda_init, pos_row, pos_col, not last)
        h_lat = h_lat + mod_lat[2] * rms_norm(y_lat, norm_g[i, 1])

        def channel_mix(u):
            if i % 2 == 0:
                return swiglu(u, ffn_w_gu[i // 2], ffn_w_down[i // 2])
            return moe_swiglu(u, moe_router[i // 2], moe_w_gu[i // 2], moe_w_down[i // 2])

        u_lat = modulate(rms_norm(h_lat, norm_g[i, 2]), mod_lat[3], mod_lat[4])
        h_lat = h_lat + mod_lat[5] * rms_norm(channel_mix(u_lat), norm_g[i, 3])
        if not last:
            h_ctx = h_ctx + mod_ctx[2] * rms_norm(y_ctx, norm_g[i, 1])
            u_ctx = modulate(rms_norm(h_ctx, norm_g[i, 2]), mod_ctx[3], mod_ctx[4])
            h_ctx = h_ctx + mod_ctx[5] * rms_norm(channel_mix(u_ctx), norm_g[i, 3])
    return h_lat
```

```python
import functools
import math

import jax
import jax.numpy as jnp
import numpy as np
from jax import lax
from jax.experimental import pallas as pl
from jax.experimental.pallas import tpu as pltpu

F32 = jnp.float32
BF16 = jnp.bfloat16

N_MIXERS = 4
N_MOD = 6
GRID_W = 64
ROPE_THETA = 10000.0
NORM_EPS = 1e-6
A_HEADS = 8
A_DK = 128
A_CHUNK = 32
B_HEADS = 16
B_Q_RANK = 384
B_KV_RANK = 256
B_NOPE = 64
B_ROPE = 32
B_V = 64
C_HEADS = 8
C_KV_HEADS = 2
C_HEAD_DIM = 128
D_HEADS = 8
D_QK_DIM = 64
N_EXPERTS = 8
TOP_K = 2

LANES = 128
V7X_VMEM_BYTES = 64 * 1024 * 1024
VMEM_LIMIT = V7X_VMEM_BYTES - 8 * 1024 * 1024

ROW_TILE = 1024
PROJ_TILE = 512
Q_TILE = 256
KEY_BLOCK = 512
GLA_TILE = 256
FF_CHUNK = 512
MOE_TILE = 512


def _cparams(*sem):
    return pltpu.CompilerParams(dimension_semantics=sem, vmem_limit_bytes=VMEM_LIMIT)


def _rms(x, g):
    return x * lax.rsqrt(jnp.mean(x * x, axis=-1, keepdims=True) + NORM_EPS) * g


def _silu(x):
    return x * jax.nn.sigmoid(x)


class Layout:
    def __init__(self, batch, seq, ctx_len, tile_cap, with_ctx=True):
        tile = math.gcd(math.gcd(seq, batch * ctx_len), tile_cap)
        self.batch, self.seq, self.ctx_len, self.tile = batch, seq, ctx_len, tile
        self.n_lat = batch * seq
        self.n_ctx = batch * ctx_len if with_ctx else 0
        self.rows = self.n_lat + self.n_ctx
        assert seq % tile == 0 and self.n_ctx % tile == 0
        self.lat_tiles = self.n_lat // tile
        self.tiles = self.rows // tile
        self.tiles_per_batch = seq // tile

    def mod_row(self, i):
        return jnp.where(i < self.lat_tiles, 1 + i // self.tiles_per_batch, 0)

    def rope_block(self, i):
        return jnp.where(i < self.lat_tiles, i % self.tiles_per_batch, self.tiles_per_batch)

    def mod_spec(self, which, d):
        return pl.BlockSpec((1, 1, d), lambda i, *_: (self.mod_row(i) * N_MOD + which, 0, 0))


def _modvec_kernel(c_ref, w_ref, b_ref, o_ref):
    cond = _silu(c_ref[...]).astype(BF16)
    o_ref[0] = jnp.dot(cond, w_ref[0].astype(BF16), preferred_element_type=F32) + b_ref[0]


def modvec(cond_rows, mod_w, mod_b):
    depth, d, n = mod_w.shape
    r = cond_rows.shape[0]
    tn = d
    return pl.pallas_call(
        _modvec_kernel,
        out_shape=jax.ShapeDtypeStruct((depth, r, n), F32),
        grid=(depth, n // tn),
        in_specs=[pl.BlockSpec((r, d), lambda l, j: (0, 0)),
                  pl.BlockSpec((1, d, tn), lambda l, j: (l, 0, j)),
                  pl.BlockSpec((1, 1, tn), lambda l, j: (l, 0, j))],
        out_specs=pl.BlockSpec((1, r, tn), lambda l, j: (l, 0, j)),
        compiler_params=_cparams("parallel", "parallel"),
        name="modvec",
    )(cond_rows, mod_w, mod_b.reshape(depth, 1, n))


def _proj_kernel(*refs, has_mod, blocks, has_gain, has_rope):
    it = iter(refs)
    x_ref, g_ref = next(it), next(it)
    sh_ref = sc_ref = hg_ref = c_ref = s_ref = None
    if has_mod:
        sh_ref, sc_ref = next(it), next(it)
    w_ref = next(it)
    if has_gain:
        hg_ref = next(it)
    if has_rope:
        c_ref, s_ref = next(it), next(it)
    o_ref, u_ref = next(it), next(it)

    @pl.when(pl.program_id(1) == 0)
    def _():
        y = _rms(x_ref[...].astype(F32), g_ref[...])
        if has_mod:
            y = y * (1 + sc_ref[0]) + sh_ref[0]
        u_ref[...] = y.astype(BF16)

    acc = jnp.dot(u_ref[...], w_ref[...], preferred_element_type=F32)
    if blocks is None:
        o_ref[...] = acc.astype(o_ref.dtype)
        return
    for c, (gain_row, rope) in enumerate(blocks):
        a = acc[:, c * LANES:(c + 1) * LANES]
        if gain_row is not None:
            a = _rms(a, hg_ref[gain_row:gain_row + 1, :])
        if rope:
            a = a * c_ref[...] + pltpu.roll(a, LANES // 2, 1) * s_ref[...]
        o_ref[:, c * LANES:(c + 1) * LANES] = a.astype(o_ref.dtype)


def project(x, g, w, lay, *, modt=None, which=None, x_block=0, tn=None, blocks=None,
            head_gain=None, rope=None, out_dtype=F32, name="project"):
    k, n = w.shape
    tm = lay.tile
    tn = n if tn is None else tn
    assert blocks is None or tn == n
    in_specs = [pl.BlockSpec((tm, k), lambda i, j: (i, x_block)),
                pl.BlockSpec((1, k), lambda i, j: (0, 0))]
    args = [x, g.reshape(1, k)]
    if modt is not None:
        in_specs += [lay.mod_spec(which[0], k), lay.mod_spec(which[1], k)]
        args += [modt, modt]
    in_specs.append(pl.BlockSpec((k, tn), lambda i, j: (0, j)))
    args.append(w)
    if head_gain is not None:
        in_specs.append(pl.BlockSpec(head_gain.shape, lambda i, j: (0, 0)))
        args.append(head_gain)
    if rope is not None:
        spec = pl.BlockSpec((tm, LANES), lambda i, j: (lay.rope_block(i), 0))
        in_specs += [spec, spec]
        args += list(rope)
    kern = functools.partial(_proj_kernel, has_mod=modt is not None, blocks=blocks,
                             has_gain=head_gain is not None, has_rope=rope is not None)
    return pl.pallas_call(
        kern,
        out_shape=jax.ShapeDtypeStruct((lay.rows, n), out_dtype),
        grid=(lay.tiles, n // tn),
        in_specs=in_specs,
        out_specs=pl.BlockSpec((tm, tn), lambda i, j: (i, j)),
        scratch_shapes=[pltpu.VMEM((tm, k), BF16)],
        compiler_params=_cparams("parallel", "arbitrary"),
        name=name,
    )(*args)


def _oproj_kernel(a_ref, w_ref, h_ref, g_ref, gate_ref, o_ref):
    y = jnp.dot(a_ref[...], w_ref[...], preferred_element_type=F32)
    o_ref[...] = h_ref[...] + gate_ref[0] * _rms(y, g_ref[...])


def _oproj_gla_kernel(o_in_ref, gz_ref, og_ref, w_ref, h_ref, g_ref, gate_ref, o_ref):
    o = o_in_ref[...]
    gz = gz_ref[...]
    parts = []
    for hd in range(A_HEADS):
        sl = slice(hd * LANES, (hd + 1) * LANES)
        parts.append((_rms(o[:, sl], og_ref[...]) * _silu(gz[:, sl])).astype(BF16))
    a = jnp.concatenate(parts, axis=1)
    y = jnp.dot(a, w_ref[...], preferred_element_type=F32)
    o_ref[...] = h_ref[...] + gate_ref[0] * _rms(y, g_ref[...])


def out_project(a, w, h, g, modt, lay, *, gla_gate=None, name="out_project"):
    k, d = w.shape
    tm = lay.tile
    row = lambda i: (i, 0)
    if gla_gate is None:
        kern = _oproj_kernel
        in_specs = [pl.BlockSpec((tm, k), row)]
        args = [a]
    else:
        z, onorm_g, gate_block = gla_gate
        kern = _oproj_gla_kernel
        in_specs = [pl.BlockSpec((tm, k), row),
                    pl.BlockSpec((tm, k), lambda i: (i, gate_block)),
                    pl.BlockSpec((1, LANES), lambda i: (0, 0))]
        args = [a, z, onorm_g.reshape(1, LANES)]
    in_specs += [pl.BlockSpec((k, d), lambda i: (0, 0)),
                 pl.BlockSpec((tm, d), row),
                 pl.BlockSpec((1, d), lambda i: (0, 0)),
                 lay.mod_spec(2, d)]
    args += [w, h, g.reshape(1, d), modt]
    return pl.pallas_call(
        kern,
        out_shape=jax.ShapeDtypeStruct((lay.rows, d), F32),
        grid=(lay.tiles,),
        in_specs=in_specs,
        out_specs=pl.BlockSpec((tm, d), row),
        compiler_params=_cparams("parallel"),
        name=name,
    )(*args)


def _ffn_kernel(h_ref, g2_ref, sh_ref, sc_ref, wg_ref, wu_ref, wd_ref, g3_ref, gate_ref, o_ref,
                u_ref, acc_ref):
    j = pl.program_id(1)

    @pl.when(j == 0)
    def _():
        y = _rms(h_ref[...], g2_ref[...]) * (1 + sc_ref[0]) + sh_ref[0]
        u_ref[...] = y.astype(BF16)
        acc_ref[...] = jnp.zeros_like(acc_ref)

    u = u_ref[...]
    a = jnp.dot(u, wg_ref[...], preferred_element_type=F32)
    b = jnp.dot(u, wu_ref[...], preferred_element_type=F32)
    hid = (_silu(a) * b).astype(BF16)
    acc_ref[...] += jnp.dot(hid, wd_ref[...], preferred_element_type=F32)

    @pl.when(j == pl.num_programs(1) - 1)
    def _():
        o_ref[...] = h_ref[...] + gate_ref[0] * _rms(acc_ref[...], g3_ref[...])


def ffn_residual(h, g2, g3, w_gu, w_down, modt, lay):
    d = h.shape[1]
    ff = w_down.shape[0]
    fc = FF_CHUNK
    nfc = ff // fc
    tm = lay.tile
    row = lambda i, j: (i, 0)
    vec = pl.BlockSpec((1, d), lambda i, j: (0, 0))
    return pl.pallas_call(
        _ffn_kernel,
        out_shape=jax.ShapeDtypeStruct((lay.rows, d), F32),
        grid=(lay.tiles, nfc),
        in_specs=[pl.BlockSpec((tm, d), row), vec, lay.mod_spec(3, d), lay.mod_spec(4, d),
                  pl.BlockSpec((d, fc), lambda i, j: (0, j)),
                  pl.BlockSpec((d, fc), lambda i, j: (0, j + nfc)),
                  pl.BlockSpec((fc, d), lambda i, j: (j, 0)),
                  vec, lay.mod_spec(5, d)],
        out_specs=pl.BlockSpec((tm, d), row),
        scratch_shapes=[pltpu.VMEM((tm, d), BF16), pltpu.VMEM((tm, d), F32)],
        compiler_params=_cparams("parallel", "arbitrary"),
        name="ffn_residual",
    )(h, g2.reshape(1, d), modt, modt, w_gu, w_gu, w_down, g3.reshape(1, d), modt)


def _router_kernel(h_ref, g2_ref, sh_ref, sc_ref, r_ref, u_ref, ids_ref, wts_ref):
    u = (_rms(h_ref[...], g2_ref[...]) * (1 + sc_ref[0]) + sh_ref[0]).astype(BF16)
    u_ref[...] = u
    logits = jnp.dot(u, r_ref[...], preferred_element_type=F32)
    lane = lax.broadcasted_iota(jnp.int32, logits.shape, 1)
    neg = jnp.float32(-jnp.inf)
    logits = jnp.where(lane < N_EXPERTS, logits, neg)
    m1 = jnp.max(logits, axis=-1, keepdims=True)
    i1 = jnp.min(jnp.where(logits == m1, lane, LANES), axis=-1, keepdims=True)
    rest = jnp.where(lane == i1, neg, logits)
    m2 = jnp.max(rest, axis=-1, keepdims=True)
    i2 = jnp.min(jnp.where(rest == m2, lane, LANES), axis=-1, keepdims=True)
    e = jnp.exp(m2 - m1)
    w1 = 1.0 / (1.0 + e)
    w2 = e / (1.0 + e)
    ids_ref[...] = jnp.where(lane == 0, i1, jnp.where(lane == 1, i2, 0))
    wts_ref[...] = jnp.where(lane == 0, w1, jnp.where(lane == 1, w2, 0.0))


def moe_router(h, g2, router, modt, lay):
    d = h.shape[1]
    tm = lay.tile
    r_pad = jnp.zeros((d, LANES), BF16).at[:, :N_EXPERTS].set(router.astype(BF16))
    row = lambda i: (i, 0)
    return pl.pallas_call(
        _router_kernel,
        out_shape=(jax.ShapeDtypeStruct((lay.rows, d), BF16),
                   jax.ShapeDtypeStruct((lay.rows, LANES), jnp.int32),
                   jax.ShapeDtypeStruct((lay.rows, LANES), F32)),
        grid=(lay.tiles,),
        in_specs=[pl.BlockSpec((tm, d), row), pl.BlockSpec((1, d), lambda i: (0, 0)),
                  lay.mod_spec(3, d), lay.mod_spec(4, d),
                  pl.BlockSpec((d, LANES), lambda i: (0, 0))],
        out_specs=(pl.BlockSpec((tm, d), row), pl.BlockSpec((tm, LANES), row),
                   pl.BlockSpec((tm, LANES), row)),
        compiler_params=_cparams("parallel"),
        name="moe_router",
    )(h, g2.reshape(1, d), modt, modt, r_pad)


def _moe_ffn_kernel(te_ref, nu_ref, x_ref, wg_ref, wu_ref, wd_ref, ws_ref, o_ref, acc_ref):
    t, j = pl.program_id(0), pl.program_id(1)
    used = t < nu_ref[0]

    @pl.when(j == 0)
    def _():
        acc_ref[...] = jnp.zeros_like(acc_ref)

    @pl.when(used)
    def _():
        x = x_ref[...]
        a = jnp.dot(x, wg_ref[0], preferred_element_type=F32)
        b = jnp.dot(x, wu_ref[0], preferred_element_type=F32)
        hid = (_silu(a) * b).astype(BF16)
        acc_ref[...] += jnp.dot(hid, wd_ref[0], preferred_element_type=F32)

    @pl.when(j == pl.num_programs(1) - 1)
    def _():
        o_ref[...] = acc_ref[...] * ws_ref[...]


def moe_ffn(x_sorted, slot_w, tile_expert, n_used, w_gu, w_down):
    n_slots, d = x_sorted.shape
    ff = w_down.shape[1]
    fc = FF_CHUNK
    nfc = ff // fc
    tm = MOE_TILE
    n_tiles = n_slots // tm

    def jj(t, j, te, nu):
        return jnp.where(t < nu[0], j, nfc - 1)

    grid_spec = pltpu.PrefetchScalarGridSpec(
        num_scalar_prefetch=2,
        grid=(n_tiles, nfc),
        in_specs=[pl.BlockSpec((tm, d), lambda t, j, te, nu: (t, 0)),
                  pl.BlockSpec((1, d, fc), lambda t, j, te, nu: (te[t], 0, jj(t, j, te, nu))),
                  pl.BlockSpec((1, d, fc), lambda t, j, te, nu: (te[t], 0, jj(t, j, te, nu) + nfc)),
                  pl.BlockSpec((1, fc, d), lambda t, j, te, nu: (te[t], jj(t, j, te, nu), 0)),
                  pl.BlockSpec((tm, 1), lambda t, j, te, nu: (t, 0))],
        out_specs=pl.BlockSpec((tm, d), lambda t, j, te, nu: (t, 0)),
        scratch_shapes=[pltpu.VMEM((tm, d), F32)],
    )
    return pl.pallas_call(
        _moe_ffn_kernel,
        out_shape=jax.ShapeDtypeStruct((n_slots, d), F32),
        grid_spec=grid_spec,
        compiler_params=_cparams("arbitrary", "arbitrary"),
        name="moe_ffn",
    )(tile_expert, n_used, x_sorted, w_gu, w_gu, w_down, slot_w)


def _combine_kernel(y_ref, h_ref, g_ref, gate_ref, o_ref):
    y = y_ref[0] + y_ref[1]
    o_ref[...] = h_ref[...] + gate_ref[0] * _rms(y, g_ref[...])


def combine_residual(y_pair, h, g3, modt, lay):
    d = h.shape[1]
    tm = lay.tile
    row = lambda i: (i, 0)
    return pl.pallas_call(
        _combine_kernel,
        out_shape=jax.ShapeDtypeStruct((lay.rows, d), F32),
        grid=(lay.tiles,),
        in_specs=[pl.BlockSpec((2, tm, d), lambda i: (0, i, 0)), pl.BlockSpec((tm, d), row),
                  pl.BlockSpec((1, d), lambda i: (0, 0)), lay.mod_spec(5, d)],
        out_specs=pl.BlockSpec((tm, d), row),
        compiler_params=_cparams("parallel"),
        name="moe_combine",
    )(y_pair, h, g3.reshape(1, d), modt)


def moe_residual(h, g2, g3, router, w_gu, w_down, modt, lay):
    rows, d = h.shape[0], h.shape[1]
    rows = lay.rows
    u, ids, wts = moe_router(h, g2, router, modt, lay)
    e_flat = ids[:, :TOP_K].reshape(-1)
    w_flat = wts[:, :TOP_K].reshape(-1)
    n = rows * TOP_K
    onehot = (e_flat[:, None] == jnp.arange(N_EXPERTS, dtype=jnp.int32)[None, :]).astype(jnp.int32)
    csum = jnp.cumsum(onehot, axis=0)
    counts = csum[-1]
    rank = jnp.take_along_axis(csum, e_flat[:, None], axis=1)[:, 0] - 1
    padded = ((counts + MOE_TILE - 1) // MOE_TILE) * MOE_TILE
    ends = jnp.cumsum(padded)
    starts = ends - padded
    dest = starts[e_flat] + rank
    n_slots = ((n + N_EXPERTS * (MOE_TILE - 1)) // MOE_TILE + 1) * MOE_TILE
    n_tiles = n_slots // MOE_TILE
    src_tok = jnp.zeros((n_slots,), jnp.int32).at[dest].set(jnp.arange(n, dtype=jnp.int32) // TOP_K)
    slot_w = jnp.zeros((n_slots,), F32).at[dest].set(w_flat)
    n_used = (ends[-1] // MOE_TILE).astype(jnp.int32).reshape(1)
    tile_start = jnp.arange(n_tiles, dtype=jnp.int32) * MOE_TILE
    tile_expert = jnp.minimum(jnp.searchsorted(ends, tile_start, side="right"),
                              N_EXPERTS - 1).astype(jnp.int32)
    last_e = jnp.max(jnp.where(counts > 0, jnp.arange(N_EXPERTS, dtype=jnp.int32), 0))
    tile_expert = jnp.where(tile_start < ends[-1], tile_expert, last_e)
    x_sorted = jnp.take(u, src_tok, axis=0)
    y_sorted = moe_ffn(x_sorted, slot_w.reshape(n_slots, 1), tile_expert, n_used, w_gu, w_down)
    y_pair = jnp.take(y_sorted, dest.reshape(rows, TOP_K).T, axis=0)
    return combine_residual(y_pair, h, g3, modt, lay)


def _gla_kernel(q_ref, ff_ref, fb_ref, v_ref, lbf_ref, lbb_ref, s0_ref, _, o_ref, st_ref,
                s_scr, ob_scr, *, n_tiles):
    rt = GLA_TILE
    nch = rt // A_CHUNK
    s_scr[...] = s0_ref[0, 0]
    row = lax.broadcasted_iota(jnp.int32, (rt, LANES), 0) % A_CHUNK
    ci = lax.broadcasted_iota(jnp.int32, (nch, A_CHUNK, A_CHUNK), 1)
    si = lax.broadcasted_iota(jnp.int32, (nch, A_CHUNK, A_CHUNK), 2)

    def direction(r0, f_ref, lb_ref, d, reverse):
        rows = pl.ds(pl.multiple_of(r0, rt), rt)
        lb = lb_ref[...]
        f = lb + (1.0 - lb) * jax.nn.sigmoid(f_ref[rows, :])
        logf = jnp.log(f)
        kk = 1.0 - f
        q = q_ref[rows, :] * (A_DK ** -0.5)
        v = v_ref[rows, :]
        cum = logf
        for s in (1, 2, 4, 8, 16):
            if reverse:
                cum = cum + jnp.where(row < A_CHUNK - s, pltpu.roll(cum, rt - s, 0), 0.0)
            else:
                cum = cum + jnp.where(row >= s, pltpu.roll(cum, s, 0), 0.0)
        cum3 = cum.reshape(nch, A_CHUNK, LANES)
        end3 = cum3[:, 0:1, :] if reverse else cum3[:, A_CHUNK - 1:A_CHUNK, :]
        q_dec = (q.reshape(nch, A_CHUNK, LANES) * jnp.exp(cum3)).astype(BF16)
        kk3 = kk.reshape(nch, A_CHUNK, LANES)
        k_neg = (kk3 * jnp.exp(-cum3)).astype(BF16)
        k_end = (kk3 * jnp.exp(end3 - cum3)).astype(BF16)
        v3 = v.reshape(nch, A_CHUNK, LANES).astype(BF16)
        decay = jnp.exp(end3)
        att = jnp.einsum("gcd,gsd->gcs", q_dec, k_neg, preferred_element_type=F32)
        keep = (si >= ci) if reverse else (si <= ci)
        att = jnp.where(keep, att, 0.0).astype(BF16)
        o_intra = jnp.einsum("gcs,gsv->gcv", att, v3, preferred_element_type=F32)
        outs = [None] * nch
        state = s_scr[d]
        for g in (range(nch - 1, -1, -1) if reverse else range(nch)):
            o_inter = lax.dot_general(q_dec[g], state.astype(BF16), (((1,), (1,)), ((), ())),
                                      preferred_element_type=F32)
            outs[g] = o_intra[g] + o_inter
            upd = lax.dot_general(v3[g], k_end[g], (((0,), (0,)), ((), ())),
                                  preferred_element_type=F32)
            state = decay[g] * state + upd
        s_scr[d] = state
        return jnp.concatenate(outs, axis=0), rows

    def body(t, carry):
        o_f, rows_f = direction(t * rt, ff_ref, lbf_ref, 0, False)
        o_ref[rows_f, :] = o_f
        o_b, rows_b = direction((n_tiles - 1 - t) * rt, fb_ref, lbb_ref, 1, True)
        ob_scr[rows_b, :] = o_b
        return carry

    lax.fori_loop(0, n_tiles, body, 0)
    o_ref[...] += ob_scr[...]
    st_ref[0, 0] = s_scr[...]


def gla_scan(z, lb_f, lb_b, s0, o_alias, *, batch, seg_len, row0):
    rows = z.shape[0]
    n_tiles = seg_len // GLA_TILE
    rb0 = row0 // seg_len
    assert row0 % seg_len == 0

    def col(kind):
        return pl.BlockSpec((seg_len, LANES), lambda b, h: (rb0 + b, kind * A_HEADS + h))

    lbspec = pl.BlockSpec((1, LANES), lambda b, h: (0, h))
    sspec = pl.BlockSpec((1, 1, 2, LANES, LANES), lambda b, h: (b, h, 0, 0, 0))
    kern = functools.partial(_gla_kernel, n_tiles=n_tiles)
    return pl.pallas_call(
        kern,
        out_shape=(jax.ShapeDtypeStruct((rows, A_HEADS * LANES), F32),
                   jax.ShapeDtypeStruct(s0.shape, F32)),
        grid=(batch, A_HEADS),
        in_specs=[col(0), col(1), col(2), col(3), lbspec, lbspec, sspec,
                  pl.BlockSpec(memory_space=pl.ANY)],
        out_specs=(pl.BlockSpec((seg_len, LANES), lambda b, h: (rb0 + b, h)), sspec),
        scratch_shapes=[pltpu.VMEM((2, LANES, LANES), F32), pltpu.VMEM((seg_len, LANES), F32)],
        input_output_aliases={7: 0},
        compiler_params=_cparams("parallel", "parallel"),
        name="gla_scan",
    )(z, z, z, z, lb_f.reshape(1, -1), lb_b.reshape(1, -1), s0, o_alias)


def _flash(q, kc, vc, kl_refs, vl_ref, scale, n_kblocks):
    nt = (((1,), (1,)), ((), ()))
    s = lax.dot_general(q, kc, nt, preferred_element_type=F32) * scale
    m = jnp.max(s, axis=-1, keepdims=True)
    p = jnp.exp(s - m)
    l = jnp.sum(p, axis=-1, keepdims=True)
    acc = jnp.dot(p.astype(BF16), vc, preferred_element_type=F32)
    if n_kblocks:
        def body(i, carry):
            m, l, acc = carry
            ks = pl.ds(pl.multiple_of(i * KEY_BLOCK, KEY_BLOCK), KEY_BLOCK)
            parts = [r[ks, :].astype(BF16) for r in kl_refs]
            kb = parts[0] if len(parts) == 1 else jnp.concatenate(parts, axis=1)
            s = lax.dot_general(q, kb, nt, preferred_element_type=F32) * scale
            m_new = jnp.maximum(m, jnp.max(s, axis=-1, keepdims=True))
            alpha = jnp.exp(m - m_new)
            p = jnp.exp(s - m_new)
            l = alpha * l + jnp.sum(p, axis=-1, keepdims=True)
            acc = alpha * acc + jnp.dot(p.astype(BF16), vl_ref[ks, :], preferred_element_type=F32)
            return m_new, l, acc
        m, l, acc = lax.fori_loop(0, n_kblocks, body, (m, l, acc))
    return acc / l


def _attn_gqa_kernel(*refs, scale, n_kblocks):
    if n_kblocks:
        q_ref, kc_ref, vc_ref, kl_ref, vl_ref, _, o_ref = refs
        kl = [kl_ref]
    else:
        q_ref, kc_ref, vc_ref, _, o_ref = refs
        kl, vl_ref = [], None
    kc, vc = kc_ref[...], vc_ref[...]
    for g in range(C_HEADS // C_KV_HEADS):
        sl = slice(g * LANES, (g + 1) * LANES)
        o_ref[:, sl] = _flash(q_ref[:, sl], kc, vc, kl, vl_ref, scale, n_kblocks).astype(o_ref.dtype)


def _attn_mla_kernel(*refs, scale, n_kblocks):
    if n_kblocks:
        qn_ref, qr_ref, knc_ref, krc_ref, vc_ref, knl_ref, krl_ref, vl_ref, _, o_ref = refs
        kl = [knl_ref, krl_ref]
    else:
        qn_ref, qr_ref, knc_ref, krc_ref, vc_ref, _, o_ref = refs
        kl, vl_ref = [], None
    kc = jnp.concatenate([knc_ref[...], krc_ref[...].astype(BF16)], axis=1)
    vc = vc_ref[...]
    qn, qr = qn_ref[...], qr_ref[...]
    lane = lax.broadcasted_iota(jnp.int32, qn.shape, 1)
    zero = jnp.zeros_like(qn)
    outs = []
    for hd in range(2):
        mn = (lane // B_NOPE) == hd
        mr = ((lane % (LANES // 2)) // (B_ROPE // 2)) == hd
        q = jnp.concatenate([jnp.where(mn, qn, zero), jnp.where(mr, qr, zero)], axis=1)
        outs.append(_flash(q, kc, vc, kl, vl_ref, scale, n_kblocks))
    o_ref[...] = jnp.where(lane < B_V, outs[0], outs[1]).astype(o_ref.dtype)


def _attn_diff_kernel(*refs, scale, n_kblocks, out_scale):
    if n_kblocks:
        q_ref, lam_ref, og_ref, kc_ref, vc_ref, kl_ref, vl_ref, _, o_ref = refs
        kl = [kl_ref]
    else:
        q_ref, lam_ref, og_ref, kc_ref, vc_ref, _, o_ref = refs
        kl, vl_ref = [], None
    kc, vc = kc_ref[...], vc_ref[...]
    q = q_ref[...]
    lane = lax.broadcasted_iota(jnp.int32, q.shape, 1)
    zero = jnp.zeros_like(q)
    maps = []
    for comp in range(2):
        mc = ((lane // (D_QK_DIM // 2)) % 2) == comp
        maps.append(_flash(jnp.where(mc, q, zero), kc, vc, kl, vl_ref, scale, n_kblocks))
    o = maps[0] - lam_ref[...] * maps[1]
    o_ref[...] = (_rms(o, og_ref[...]) * out_scale).astype(o_ref.dtype)


def attention(kind, srcs, o_alias, *, batch, seq, ctx_len, queries, scale, extra=(), out_scale=1.0):
    n_lat = batch * seq
    lat = queries == "lat"
    tq = Q_TILE
    nq = (seq if lat else ctx_len) // tq
    q0 = 0 if lat else n_lat // tq
    c0 = n_lat // ctx_len
    n_kblocks = seq // KEY_BLOCK if lat else 0

    def qspec(width, colfn):
        return pl.BlockSpec((tq, width), lambda b, g, i: (q0 + b * nq + i, colfn(g)))

    def cspec(colfn):
        return pl.BlockSpec((ctx_len, LANES), lambda b, g, i: (c0 + b, colfn(g)))

    def lspec(colfn):
        return pl.BlockSpec((seq, LANES), lambda b, g, i: (b, colfn(g)))

    if kind == "gqa":
        (qkv,) = srcs
        groups = C_KV_HEADS
        gw = (C_HEADS // C_KV_HEADS) * LANES
        kcol = lambda g: C_HEADS + g
        vcol = lambda g: C_HEADS + C_KV_HEADS + g
        in_specs = [qspec(gw, lambda g: g), cspec(kcol), cspec(vcol)]
        args = [qkv, qkv, qkv]
        if lat:
            in_specs += [lspec(kcol), lspec(vcol)]
            args += [qkv, qkv]
        kern = functools.partial(_attn_gqa_kernel, scale=scale, n_kblocks=n_kblocks)
        ow = gw
    elif kind == "mla":
        q_all, kv_all, cin = srcs
        groups = B_HEADS // 2
        kr_col = lambda g: B_Q_RANK // LANES
        in_specs = [qspec(LANES, lambda g: g), qspec(LANES, lambda g: groups + g),
                    cspec(lambda g: g), cspec(kr_col), cspec(lambda g: groups + g)]
        args = [q_all, q_all, kv_all, cin, kv_all]
        if lat:
            in_specs += [lspec(lambda g: g), lspec(kr_col), lspec(lambda g: groups + g)]
            args += [kv_all, cin, kv_all]
        kern = functools.partial(_attn_mla_kernel, scale=scale, n_kblocks=n_kblocks)
        ow = LANES
    else:
        (qkv,) = srcs
        lam, onorm_g = extra
        groups = D_HEADS
        kcol = lambda g: D_HEADS + g
        vcol = lambda g: 2 * D_HEADS + g
        vec = pl.BlockSpec((1, LANES), lambda b, g, i: (0, 0))
        in_specs = [qspec(LANES, lambda g: g), vec, vec, cspec(kcol), cspec(vcol)]
        args = [qkv, lam, onorm_g.reshape(1, LANES), qkv, qkv]
        if lat:
            in_specs += [lspec(kcol), lspec(vcol)]
            args += [qkv, qkv]
        kern = functools.partial(_attn_diff_kernel, scale=scale, n_kblocks=n_kblocks,
                                 out_scale=out_scale)
        ow = LANES
    in_specs.append(pl.BlockSpec(memory_space=pl.ANY))
    args.append(o_alias)
    return pl.pallas_call(
        kern,
        out_shape=jax.ShapeDtypeStruct(o_alias.shape, o_alias.dtype),
        grid=(batch, groups, nq),
        in_specs=in_specs,
        out_specs=pl.BlockSpec((tq, ow), lambda b, g, i: (q0 + b * nq + i, g)),
        input_output_aliases={len(args) - 1: 0},
        compiler_params=_cparams("parallel", "parallel", "parallel"),
        name=f"attn_{kind}_{queries}",
    )(*args)


def _rope_tables(seq, tile, n_pairs, lane_slots):
    t = np.arange(seq)
    pos = np.stack([t // GRID_W, t % GRID_W], axis=0).astype(np.float32)
    inv = jnp.asarray(ROPE_THETA, F32) ** (-jnp.arange(0, 2 * n_pairs, 2, dtype=F32) / (2 * n_pairs))
    ang = jnp.asarray(pos)[:, :, None] * inv[None, None, :]
    cos, sin = jnp.cos(ang), jnp.sin(ang)
    ccols, scols = [], []
    for slot in lane_slots:
        if slot is None:
            ccols.append(jnp.ones((seq,), F32))
            scols.append(jnp.zeros((seq,), F32))
        else:
            axis, fi, sign = slot
            ccols.append(cos[axis, :, fi])
            scols.append(sign * sin[axis, :, fi])
    ctab = jnp.concatenate([jnp.stack(ccols, axis=1), jnp.ones((tile, LANES), F32)], axis=0)
    stab = jnp.concatenate([jnp.stack(scols, axis=1), jnp.zeros((tile, LANES), F32)], axis=0)
    return ctab, stab


def _gqa_layout():
    q = C_HEAD_DIM // 4
    perm = np.concatenate([np.arange(0, q), np.arange(2 * q, 3 * q), np.arange(q, 2 * q),
                           np.arange(3 * q, 4 * q)])
    slots = [(axis, j, sign) for sign in (-1.0, 1.0) for axis in (0, 1) for j in range(q)]
    return perm, slots


def _diff_layout():
    e = D_QK_DIM // 4
    perm, slots = [], []
    for half, sign in ((0, -1.0), (1, 1.0)):
        for comp in range(2):
            for axis in range(2):
                perm.extend(comp * D_QK_DIM + axis * 2 * e + half * e + j for j in range(e))
                slots.extend((axis, j, sign) for j in range(e))
    return np.asarray(perm), slots


def _mla_rope_layout():
    e = B_ROPE // 4
    dims, slots = [], []
    for half, sign in ((0, -1.0), (1, 1.0)):
        for hd in range(2):
            for axis in range(2):
                dims.extend((hd, axis * 2 * e + half * e + j) for j in range(e))
                slots.extend((axis, j, sign) for j in range(e))
        dims.extend([None] * (2 * B_ROPE // 2))
        slots.extend([None] * (2 * B_ROPE // 2))
    return dims, slots


def _gather_cols(w, cols):
    idx = np.asarray([0 if c is None else c for c in cols])
    keep = np.asarray([c is not None for c in cols])
    out = jnp.take(w, jnp.asarray(idx), axis=1)
    return jnp.where(jnp.asarray(keep)[None, :], out, 0).astype(BF16)


def kernel(x, c, ctx, c_ctx, mod_w, mod_b, norm_g, a_w_in, a_lb, a_onorm_g, a_w_out, b_w_in, b_qnorm_g, b_kvnorm_g, b_w_qb, b_w_kvb, b_w_out, c_w_in, c_qnorm_g, c_knorm_g, c_w_out, d_w_in, d_lambda, d_onorm_g, d_w_out, ffn_w_gu, ffn_w_down, moe_router, moe_w_gu, moe_w_down):
    batch, seq, d = x.shape
    ctx_len = ctx.shape[1]
    depth = mod_w.shape[0]
    n_lat = batch * seq
    lay_all = Layout(batch, seq, ctx_len, ROW_TILE, with_ctx=True)
    lay_lat = Layout(batch, seq, ctx_len, ROW_TILE, with_ctx=False)
    lay_proj = Layout(batch, seq, ctx_len, PROJ_TILE, with_ctx=True)
    rows = lay_all.rows

    h = jnp.concatenate([x.reshape(n_lat, d), ctx.reshape(batch * ctx_len, d)], axis=0)

    n_rows = 8 * ((batch + 1 + 7) // 8)
    cond = jnp.zeros((n_rows, d), F32).at[0].set(c_ctx).at[1:batch + 1].set(c)
    mod_all = modvec(cond, mod_w, mod_b)

    lower_bounds = jnp.cumsum(jax.nn.softmax(a_lb.astype(F32), axis=1), axis=1)

    for i in range(depth):
        last = i == depth - 1
        kind, j = i % N_MIXERS, i // N_MIXERS
        modt = mod_all[i].reshape(n_rows * N_MOD, 1, d)
        lay_out = lay_lat if last else lay_all
        g0, g1, g2, g3 = norm_g[i, 0], norm_g[i, 1], norm_g[i, 2], norm_g[i, 3]

        if kind == 0:
            z = project(h, g0, a_w_in[j].astype(BF16), lay_proj, modt=modt, which=(0, 1),
                        tn=A_HEADS * LANES, name="gla_in")
            o = jnp.zeros((rows, A_HEADS * LANES), F32)
            s0 = jnp.zeros((batch, A_HEADS, 2, LANES, LANES), F32)
            o, s_ctx = gla_scan(z, lower_bounds[0, i], lower_bounds[1, i], s0, o,
                                batch=batch, seg_len=ctx_len, row0=n_lat)
            o, _ = gla_scan(z, lower_bounds[0, i], lower_bounds[1, i], s_ctx, o,
                            batch=batch, seg_len=seq, row0=0)
            h = out_project(o, a_w_out[j].astype(BF16), h, g1, modt, lay_out,
                            gla_gate=(z, a_onorm_g[j], 4), name="gla_out")
        else:
            if kind == 1:
                dims, slots = _mla_rope_layout()
                rope = _rope_tables(seq, lay_proj.tile,B_ROPE // 4, slots)
                w_in = b_w_in[j]
                kr_cols = [None if dd is None else B_Q_RANK + B_KV_RANK + dd[1] for dd in dims]
                w_cat = jnp.concatenate(
                    [w_in[:, :B_Q_RANK].astype(BF16), _gather_cols(w_in, kr_cols),
                     w_in[:, B_Q_RANK:B_Q_RANK + B_KV_RANK].astype(BF16)], axis=1)
                nq_blocks = B_Q_RANK // LANES
                nkv_blocks = B_KV_RANK // LANES
                blocks = [(None, False)] * nq_blocks + [(None, True)] + [(None, False)] * nkv_blocks
                cin = project(h, g0, w_cat, lay_proj, modt=modt, which=(0, 1), blocks=blocks,
                              rope=rope, name="mla_in")
                hw = B_NOPE + B_ROPE
                qn_cols = [hd * hw + t for hd in range(B_HEADS) for t in range(B_NOPE)]
                qr_cols = [None if dd is None else (2 * p + dd[0]) * hw + B_NOPE + dd[1]
                           for p in range(B_HEADS // 2) for dd in dims]
                w_q = _gather_cols(b_w_qb[j], qn_cols + qr_cols)
                n_blk = B_HEADS * B_NOPE // LANES
                q_all = project(cin, b_qnorm_g[j], w_q, lay_proj, x_block=0,
                                blocks=[(None, False)] * n_blk + [(None, True)] * (B_HEADS // 2),
                                rope=rope, out_dtype=BF16, name="mla_q")
                kw = B_NOPE + B_V
                kn_cols = [hd * kw + t for hd in range(B_HEADS) for t in range(B_NOPE)]
                v_cols = [hd * kw + B_NOPE + t for hd in range(B_HEADS) for t in range(B_V)]
                w_kv = _gather_cols(b_w_kvb[j], kn_cols + v_cols)
                kv_all = project(cin, b_kvnorm_g[j], w_kv, lay_proj,
                                 x_block=(B_Q_RANK + LANES) // B_KV_RANK, out_dtype=BF16,
                                 name="mla_kv")
                srcs, akind, extra, out_scale = (q_all, kv_all, cin), "mla", (), 1.0
                scale = (B_NOPE + B_ROPE) ** -0.5
                w_out = b_w_out[j]
            elif kind == 2:
                perm, slots = _gqa_layout()
                rope = _rope_tables(seq, lay_proj.tile,C_HEAD_DIM // 4, slots)
                nqk = C_HEADS + C_KV_HEADS
                cols = [hd * C_HEAD_DIM + p for hd in range(nqk) for p in perm]
                cols += list(range(nqk * C_HEAD_DIM, (nqk + C_KV_HEADS) * C_HEAD_DIM))
                w_cat = _gather_cols(c_w_in[j], cols)
                gains = jnp.zeros((8, LANES), F32).at[0].set(c_qnorm_g[j][perm]).at[1].set(
                    c_knorm_g[j][perm])
                blocks = ([(0, True)] * C_HEADS + [(1, True)] * C_KV_HEADS
                          + [(None, False)] * C_KV_HEADS)
                qkv = project(h, g0, w_cat, lay_proj, modt=modt, which=(0, 1), blocks=blocks,
                              head_gain=gains, rope=rope, out_dtype=BF16, name="gqa_in")
                srcs, akind, extra, out_scale = (qkv,), "gqa", (), 1.0
                scale = C_HEAD_DIM ** -0.5
                w_out = c_w_out[j]
            else:
                perm, slots = _diff_layout()
                rope = _rope_tables(seq, lay_proj.tile,D_QK_DIM // 4, slots)
                nqk = 2 * D_HEADS
                cols = [hd * 2 * D_QK_DIM + p for hd in range(nqk) for p in perm]
                cols += list(range(nqk * 2 * D_QK_DIM, (nqk + D_HEADS) * 2 * D_QK_DIM))
                w_cat = _gather_cols(d_w_in[j], cols)
                blocks = [(None, True)] * nqk + [(None, False)] * D_HEADS
                qkv = project(h, g0, w_cat, lay_proj, modt=modt, which=(0, 1), blocks=blocks,
                              rope=rope, out_dtype=BF16, name="diff_in")
                lambda_init = 0.8 - 0.6 * math.exp(-0.3 * i)
                lam = d_lambda[j].astype(F32)
                lam_full = (jnp.exp(jnp.sum(lam[0] * lam[1])) - jnp.exp(jnp.sum(lam[2] * lam[3]))
                            + lambda_init)
                extra = (jnp.full((1, LANES), lam_full, F32), d_onorm_g[j])
                srcs, akind, out_scale = (qkv,), "diff", 1.0 - lambda_init
                scale = D_QK_DIM ** -0.5
                w_out = d_w_out[j]
            a = jnp.zeros((rows, d), BF16)
            a = attention(akind, srcs, a, batch=batch, seq=seq, ctx_len=ctx_len, queries="lat",
                          scale=scale, extra=extra, out_scale=out_scale)
            if not last:
                a = attention(akind, srcs, a, batch=batch, seq=seq, ctx_len=ctx_len,
                              queries="ctx", scale=scale, extra=extra, out_scale=out_scale)
            h = out_project(a, w_out.astype(BF16), h, g1, modt, lay_out, name=f"{akind}_out")

        if i % 2 == 0:
            h = ffn_residual(h, g2, g3, ffn_w_gu[i // 2].astype(BF16),
                             ffn_w_down[i // 2].astype(BF16), modt, lay_out)
        else:
            h = moe_residual(h, g2, g3, moe_router[i // 2], moe_w_gu[i // 2].astype(BF16),
                             moe_w_down[i // 2].astype(BF16), modt, lay_out)
    return h[:n_lat].reshape(batch, seq, d)
```

```python
import functools
import math

import jax
import jax.numpy as jnp
import numpy as np
from jax import lax
from jax.experimental import pallas as pl
from jax.experimental.pallas import tpu as pltpu

F32 = jnp.float32
BF16 = jnp.bfloat16

N_MIXERS = 4
N_MOD = 6
GRID_W = 64
ROPE_THETA = 10000.0
NORM_EPS = 1e-6
A_HEADS = 8
A_DK = 128
A_CHUNK = 32
B_HEADS = 16
B_Q_RANK = 384
B_KV_RANK = 256
B_NOPE = 64
B_ROPE = 32
B_V = 64
C_HEADS = 8
C_KV_HEADS = 2
C_HEAD_DIM = 128
D_HEADS = 8
D_QK_DIM = 64
N_EXPERTS = 8
TOP_K = 2

LANES = 128
V7X_VMEM_BYTES = 64 * 1024 * 1024
VMEM_LIMIT = V7X_VMEM_BYTES - 8 * 1024 * 1024

ROW_TILE = 1024
PROJ_TILE = 512
Q_TILE = {"gqa": 256, "mla": 512, "diff": 512}
LOG2E = 1.4426950408889634
KEY_BLOCK = 512
GLA_TILE = 256
FF_CHUNK = 512
MOE_TILE = 512


def _cparams(*sem):
    return pltpu.CompilerParams(dimension_semantics=sem, vmem_limit_bytes=VMEM_LIMIT)


def _rms(x, g):
    return x * lax.rsqrt(jnp.mean(x * x, axis=-1, keepdims=True) + NORM_EPS) * g


def _silu(x):
    return x * jax.nn.sigmoid(x)


class Layout:
    def __init__(self, batch, seq, ctx_len, tile_cap, with_ctx=True):
        tile = math.gcd(math.gcd(seq, batch * ctx_len), tile_cap)
        self.batch, self.seq, self.ctx_len, self.tile = batch, seq, ctx_len, tile
        self.n_lat = batch * seq
        self.n_ctx = batch * ctx_len if with_ctx else 0
        self.rows = self.n_lat + self.n_ctx
        assert seq % tile == 0 and self.n_ctx % tile == 0
        self.lat_tiles = self.n_lat // tile
        self.tiles = self.rows // tile
        self.tiles_per_batch = seq // tile

    def mod_row(self, i):
        return jnp.where(i < self.lat_tiles, 1 + i // self.tiles_per_batch, 0)

    def rope_block(self, i):
        return jnp.where(i < self.lat_tiles, i % self.tiles_per_batch, self.tiles_per_batch)

    def mod_spec(self, which, d):
        return pl.BlockSpec((1, 1, d), lambda i, *_: (self.mod_row(i) * N_MOD + which, 0, 0))


def _modvec_kernel(c_ref, w_ref, b_ref, o_ref):
    cond = _silu(c_ref[...]).astype(BF16)
    o_ref[0] = jnp.dot(cond, w_ref[0].astype(BF16), preferred_element_type=F32) + b_ref[0]


def modvec(cond_rows, mod_w, mod_b):
    depth, d, n = mod_w.shape
    r = cond_rows.shape[0]
    tn = d
    return pl.pallas_call(
        _modvec_kernel,
        out_shape=jax.ShapeDtypeStruct((depth, r, n), F32),
        grid=(depth, n // tn),
        in_specs=[pl.BlockSpec((r, d), lambda l, j: (0, 0)),
                  pl.BlockSpec((1, d, tn), lambda l, j: (l, 0, j)),
                  pl.BlockSpec((1, 1, tn), lambda l, j: (l, 0, j))],
        out_specs=pl.BlockSpec((1, r, tn), lambda l, j: (l, 0, j)),
        compiler_params=_cparams("parallel", "parallel"),
        name="modvec",
    )(cond_rows, mod_w, mod_b.reshape(depth, 1, n))


def _proj_kernel(*refs, has_mod, blocks, has_gain, has_rope):
    it = iter(refs)
    x_ref, g_ref = next(it), next(it)
    sh_ref = sc_ref = hg_ref = c_ref = s_ref = None
    if has_mod:
        sh_ref, sc_ref = next(it), next(it)
    w_ref = next(it)
    if has_gain:
        hg_ref = next(it)
    if has_rope:
        c_ref, s_ref = next(it), next(it)
    o_ref, u_ref = next(it), next(it)

    @pl.when(pl.program_id(1) == 0)
    def _():
        y = _rms(x_ref[...].astype(F32), g_ref[...])
        if has_mod:
            y = y * (1 + sc_ref[0]) + sh_ref[0]
        u_ref[...] = y.astype(BF16)

    acc = jnp.dot(u_ref[...], w_ref[...], preferred_element_type=F32)
    if blocks is None:
        o_ref[...] = acc.astype(o_ref.dtype)
        return
    for c, (gain_row, rope, mult) in enumerate(blocks):
        a = acc[:, c * LANES:(c + 1) * LANES]
        if gain_row is not None:
            a = _rms(a, hg_ref[gain_row:gain_row + 1, :])
        if rope:
            a = a * c_ref[...] + pltpu.roll(a, LANES // 2, 1) * s_ref[...]
        if mult is not None:
            a = a * mult
        o_ref[:, c * LANES:(c + 1) * LANES] = a.astype(o_ref.dtype)


def project(x, g, w, lay, *, modt=None, which=None, x_block=0, tn=None, blocks=None,
            head_gain=None, rope=None, out_dtype=F32, name="project"):
    k, n = w.shape
    tm = lay.tile
    tn = n if tn is None else tn
    assert blocks is None or tn == n
    in_specs = [pl.BlockSpec((tm, k), lambda i, j: (i, x_block)),
                pl.BlockSpec((1, k), lambda i, j: (0, 0))]
    args = [x, g.reshape(1, k)]
    if modt is not None:
        in_specs += [lay.mod_spec(which[0], k), lay.mod_spec(which[1], k)]
        args += [modt, modt]
    in_specs.append(pl.BlockSpec((k, tn), lambda i, j: (0, j)))
    args.append(w)
    if head_gain is not None:
        in_specs.append(pl.BlockSpec(head_gain.shape, lambda i, j: (0, 0)))
        args.append(head_gain)
    if rope is not None:
        spec = pl.BlockSpec((tm, LANES), lambda i, j: (lay.rope_block(i), 0))
        in_specs += [spec, spec]
        args += list(rope)
    kern = functools.partial(_proj_kernel, has_mod=modt is not None, blocks=blocks,
                             has_gain=head_gain is not None, has_rope=rope is not None)
    return pl.pallas_call(
        kern,
        out_shape=jax.ShapeDtypeStruct((lay.rows, n), out_dtype),
        grid=(lay.tiles, n // tn),
        in_specs=in_specs,
        out_specs=pl.BlockSpec((tm, tn), lambda i, j: (i, j)),
        scratch_shapes=[pltpu.VMEM((tm, k), BF16)],
        compiler_params=_cparams("parallel", "arbitrary"),
        name=name,
    )(*args)


def _oproj_kernel(a_ref, w_ref, h_ref, g_ref, gate_ref, o_ref):
    y = jnp.dot(a_ref[...], w_ref[...], preferred_element_type=F32)
    o_ref[...] = h_ref[...] + gate_ref[0] * _rms(y, g_ref[...])


def _oproj_gla_kernel(o_in_ref, gz_ref, og_ref, w_ref, h_ref, g_ref, gate_ref, o_ref):
    o = o_in_ref[...]
    gz = gz_ref[...]
    parts = []
    for hd in range(A_HEADS):
        sl = slice(hd * LANES, (hd + 1) * LANES)
        parts.append((_rms(o[:, sl], og_ref[...]) * _silu(gz[:, sl])).astype(BF16))
    a = jnp.concatenate(parts, axis=1)
    y = jnp.dot(a, w_ref[...], preferred_element_type=F32)
    o_ref[...] = h_ref[...] + gate_ref[0] * _rms(y, g_ref[...])


def out_project(a, w, h, g, modt, lay, *, gla_gate=None, name="out_project"):
    k, d = w.shape
    tm = lay.tile
    row = lambda i: (i, 0)
    if gla_gate is None:
        kern = _oproj_kernel
        in_specs = [pl.BlockSpec((tm, k), row)]
        args = [a]
    else:
        z, onorm_g, gate_block = gla_gate
        kern = _oproj_gla_kernel
        in_specs = [pl.BlockSpec((tm, k), row),
                    pl.BlockSpec((tm, k), lambda i: (i, gate_block)),
                    pl.BlockSpec((1, LANES), lambda i: (0, 0))]
        args = [a, z, onorm_g.reshape(1, LANES)]
    in_specs += [pl.BlockSpec((k, d), lambda i: (0, 0)),
                 pl.BlockSpec((tm, d), row),
                 pl.BlockSpec((1, d), lambda i: (0, 0)),
                 lay.mod_spec(2, d)]
    args += [w, h, g.reshape(1, d), modt]
    return pl.pallas_call(
        kern,
        out_shape=jax.ShapeDtypeStruct((lay.rows, d), F32),
        grid=(lay.tiles,),
        in_specs=in_specs,
        out_specs=pl.BlockSpec((tm, d), row),
        compiler_params=_cparams("parallel"),
        name=name,
    )(*args)


def _ffn_kernel(h_ref, g2_ref, sh_ref, sc_ref, wg_ref, wu_ref, wd_ref, g3_ref, gate_ref, o_ref,
                u_ref, acc_ref):
    j = pl.program_id(1)

    @pl.when(j == 0)
    def _():
        y = _rms(h_ref[...], g2_ref[...]) * (1 + sc_ref[0]) + sh_ref[0]
        u_ref[...] = y.astype(BF16)
        acc_ref[...] = jnp.zeros_like(acc_ref)

    u = u_ref[...]
    a = jnp.dot(u, wg_ref[...], preferred_element_type=F32)
    b = jnp.dot(u, wu_ref[...], preferred_element_type=F32)
    hid = (_silu(a) * b).astype(BF16)
    acc_ref[...] += jnp.dot(hid, wd_ref[...], preferred_element_type=F32)

    @pl.when(j == pl.num_programs(1) - 1)
    def _():
        o_ref[...] = h_ref[...] + gate_ref[0] * _rms(acc_ref[...], g3_ref[...])


def ffn_residual(h, g2, g3, w_gu, w_down, modt, lay):
    d = h.shape[1]
    ff = w_down.shape[0]
    fc = FF_CHUNK
    nfc = ff // fc
    tm = lay.tile
    row = lambda i, j: (i, 0)
    vec = pl.BlockSpec((1, d), lambda i, j: (0, 0))
    return pl.pallas_call(
        _ffn_kernel,
        out_shape=jax.ShapeDtypeStruct((lay.rows, d), F32),
        grid=(lay.tiles, nfc),
        in_specs=[pl.BlockSpec((tm, d), row), vec, lay.mod_spec(3, d), lay.mod_spec(4, d),
                  pl.BlockSpec((d, fc), lambda i, j: (0, j)),
                  pl.BlockSpec((d, fc), lambda i, j: (0, j + nfc)),
                  pl.BlockSpec((fc, d), lambda i, j: (j, 0)),
                  vec, lay.mod_spec(5, d)],
        out_specs=pl.BlockSpec((tm, d), row),
        scratch_shapes=[pltpu.VMEM((tm, d), BF16), pltpu.VMEM((tm, d), F32)],
        compiler_params=_cparams("parallel", "arbitrary"),
        name="ffn_residual",
    )(h, g2.reshape(1, d), modt, modt, w_gu, w_gu, w_down, g3.reshape(1, d), modt)


def _router_kernel(h_ref, g2_ref, sh_ref, sc_ref, r_ref, u_ref, ids_ref, wts_ref):
    u = (_rms(h_ref[...], g2_ref[...]) * (1 + sc_ref[0]) + sh_ref[0]).astype(BF16)
    u_ref[...] = u
    logits = jnp.dot(u, r_ref[...], preferred_element_type=F32)
    lane = lax.broadcasted_iota(jnp.int32, logits.shape, 1)
    neg = jnp.float32(-jnp.inf)
    logits = jnp.where(lane < N_EXPERTS, logits, neg)
    m1 = jnp.max(logits, axis=-1, keepdims=True)
    i1 = jnp.min(jnp.where(logits == m1, lane, LANES), axis=-1, keepdims=True)
    rest = jnp.where(lane == i1, neg, logits)
    m2 = jnp.max(rest, axis=-1, keepdims=True)
    i2 = jnp.min(jnp.where(rest == m2, lane, LANES), axis=-1, keepdims=True)
    e = jnp.exp(m2 - m1)
    w1 = 1.0 / (1.0 + e)
    w2 = e / (1.0 + e)
    ids_ref[...] = jnp.where(lane == 0, i1, jnp.where(lane == 1, i2, 0))
    wts_ref[...] = jnp.where(lane == 0, w1, jnp.where(lane == 1, w2, 0.0))


def moe_router(h, g2, router, modt, lay):
    d = h.shape[1]
    tm = lay.tile
    r_pad = jnp.zeros((d, LANES), BF16).at[:, :N_EXPERTS].set(router.astype(BF16))
    row = lambda i: (i, 0)
    return pl.pallas_call(
        _router_kernel,
        out_shape=(jax.ShapeDtypeStruct((lay.rows, d), BF16),
                   jax.ShapeDtypeStruct((lay.rows, LANES), jnp.int32),
                   jax.ShapeDtypeStruct((lay.rows, LANES), F32)),
        grid=(lay.tiles,),
        in_specs=[pl.BlockSpec((tm, d), row), pl.BlockSpec((1, d), lambda i: (0, 0)),
                  lay.mod_spec(3, d), lay.mod_spec(4, d),
                  pl.BlockSpec((d, LANES), lambda i: (0, 0))],
        out_specs=(pl.BlockSpec((tm, d), row), pl.BlockSpec((tm, LANES), row),
                   pl.BlockSpec((tm, LANES), row)),
        compiler_params=_cparams("parallel"),
        name="moe_router",
    )(h, g2.reshape(1, d), modt, modt, r_pad)


def _moe_ffn_kernel(te_ref, nu_ref, x_ref, wg_ref, wu_ref, wd_ref, ws_ref, o_ref, acc_ref):
    t, j = pl.program_id(0), pl.program_id(1)
    used = t < nu_ref[0]

    @pl.when(j == 0)
    def _():
        acc_ref[...] = jnp.zeros_like(acc_ref)

    @pl.when(used)
    def _():
        x = x_ref[...]
        a = jnp.dot(x, wg_ref[0], preferred_element_type=F32)
        b = jnp.dot(x, wu_ref[0], preferred_element_type=F32)
        hid = (_silu(a) * b).astype(BF16)
        acc_ref[...] += jnp.dot(hid, wd_ref[0], preferred_element_type=F32)

    @pl.when(j == pl.num_programs(1) - 1)
    def _():
        o_ref[...] = acc_ref[...] * ws_ref[...]


def moe_ffn(x_sorted, slot_w, tile_expert, n_used, w_gu, w_down):
    n_slots, d = x_sorted.shape
    ff = w_down.shape[1]
    fc = FF_CHUNK
    nfc = ff // fc
    tm = MOE_TILE
    n_tiles = n_slots // tm

    def jj(t, j, te, nu):
        return jnp.where(t < nu[0], j, nfc - 1)

    grid_spec = pltpu.PrefetchScalarGridSpec(
        num_scalar_prefetch=2,
        grid=(n_tiles, nfc),
        in_specs=[pl.BlockSpec((tm, d), lambda t, j, te, nu: (t, 0)),
                  pl.BlockSpec((1, d, fc), lambda t, j, te, nu: (te[t], 0, jj(t, j, te, nu))),
                  pl.BlockSpec((1, d, fc), lambda t, j, te, nu: (te[t], 0, jj(t, j, te, nu) + nfc)),
                  pl.BlockSpec((1, fc, d), lambda t, j, te, nu: (te[t], jj(t, j, te, nu), 0)),
                  pl.BlockSpec((tm, 1), lambda t, j, te, nu: (t, 0))],
        out_specs=pl.BlockSpec((tm, d), lambda t, j, te, nu: (t, 0)),
        scratch_shapes=[pltpu.VMEM((tm, d), F32)],
    )
    return pl.pallas_call(
        _moe_ffn_kernel,
        out_shape=jax.ShapeDtypeStruct((n_slots, d), F32),
        grid_spec=grid_spec,
        compiler_params=_cparams("arbitrary", "arbitrary"),
        name="moe_ffn",
    )(tile_expert, n_used, x_sorted, w_gu, w_gu, w_down, slot_w)


def _combine_kernel(y_ref, h_ref, g_ref, gate_ref, o_ref):
    y = y_ref[0] + y_ref[1]
    o_ref[...] = h_ref[...] + gate_ref[0] * _rms(y, g_ref[...])


def combine_residual(y_pair, h, g3, modt, lay):
    d = h.shape[1]
    tm = lay.tile
    row = lambda i: (i, 0)
    return pl.pallas_call(
        _combine_kernel,
        out_shape=jax.ShapeDtypeStruct((lay.rows, d), F32),
        grid=(lay.tiles,),
        in_specs=[pl.BlockSpec((2, tm, d), lambda i: (0, i, 0)), pl.BlockSpec((tm, d), row),
                  pl.BlockSpec((1, d), lambda i: (0, 0)), lay.mod_spec(5, d)],
        out_specs=pl.BlockSpec((tm, d), row),
        compiler_params=_cparams("parallel"),
        name="moe_combine",
    )(y_pair, h, g3.reshape(1, d), modt)


def moe_residual(h, g2, g3, router, w_gu, w_down, modt, lay):
    rows, d = h.shape[0], h.shape[1]
    rows = lay.rows
    u, ids, wts = moe_router(h, g2, router, modt, lay)
    e_flat = ids[:, :TOP_K].reshape(-1)
    w_flat = wts[:, :TOP_K].reshape(-1)
    n = rows * TOP_K
    onehot = (e_flat[:, None] == jnp.arange(N_EXPERTS, dtype=jnp.int32)[None, :]).astype(jnp.int32)
    csum = jnp.cumsum(onehot, axis=0)
    counts = csum[-1]
    rank = jnp.take_along_axis(csum, e_flat[:, None], axis=1)[:, 0] - 1
    padded = ((counts + MOE_TILE - 1) // MOE_TILE) * MOE_TILE
    ends = jnp.cumsum(padded)
    starts = ends - padded
    dest = starts[e_flat] + rank
    n_slots = ((n + N_EXPERTS * (MOE_TILE - 1)) // MOE_TILE + 1) * MOE_TILE
    n_tiles = n_slots // MOE_TILE
    src_tok = jnp.zeros((n_slots,), jnp.int32).at[dest].set(jnp.arange(n, dtype=jnp.int32) // TOP_K)
    slot_w = jnp.zeros((n_slots,), F32).at[dest].set(w_flat)
    n_used = (ends[-1] // MOE_TILE).astype(jnp.int32).reshape(1)
    tile_start = jnp.arange(n_tiles, dtype=jnp.int32) * MOE_TILE
    tile_expert = jnp.minimum(
        jnp.sum((ends[None, :] <= tile_start[:, None]).astype(jnp.int32), axis=1), N_EXPERTS - 1)
    last_e = jnp.max(jnp.where(counts > 0, jnp.arange(N_EXPERTS, dtype=jnp.int32), 0))
    tile_expert = jnp.where(tile_start < ends[-1], tile_expert, last_e)
    x_sorted = jnp.take(u, src_tok, axis=0)
    y_sorted = moe_ffn(x_sorted, slot_w.reshape(n_slots, 1), tile_expert, n_used, w_gu, w_down)
    y_pair = jnp.take(y_sorted, dest.reshape(rows, TOP_K).T, axis=0)
    return combine_residual(y_pair, h, g3, modt, lay)


def _gla_kernel(q_ref, ff_ref, fb_ref, v_ref, lbf_ref, lbb_ref, s0_ref, _, o_ref, st_ref,
                s_scr, ob_scr, *, n_tiles):
    rt = GLA_TILE
    nch = rt // A_CHUNK
    s_scr[...] = s0_ref[0, 0]
    row = lax.broadcasted_iota(jnp.int32, (rt, LANES), 0) % A_CHUNK
    ci = lax.broadcasted_iota(jnp.int32, (nch, A_CHUNK, A_CHUNK), 1)
    si = lax.broadcasted_iota(jnp.int32, (nch, A_CHUNK, A_CHUNK), 2)

    def direction(r0, f_ref, lb_ref, d, reverse):
        rows = pl.ds(pl.multiple_of(r0, rt), rt)
        lb = lb_ref[...]
        f = lb + (1.0 - lb) * jax.nn.sigmoid(f_ref[rows, :])
        logf = jnp.log(f)
        kk = 1.0 - f
        q = q_ref[rows, :] * (A_DK ** -0.5)
        v = v_ref[rows, :]
        cum = logf
        for s in (1, 2, 4, 8, 16):
            if reverse:
                cum = cum + jnp.where(row < A_CHUNK - s, pltpu.roll(cum, rt - s, 0), 0.0)
            else:
                cum = cum + jnp.where(row >= s, pltpu.roll(cum, s, 0), 0.0)
        cum3 = cum.reshape(nch, A_CHUNK, LANES)
        end3 = cum3[:, 0:1, :] if reverse else cum3[:, A_CHUNK - 1:A_CHUNK, :]
        q_dec = (q.reshape(nch, A_CHUNK, LANES) * jnp.exp(cum3)).astype(BF16)
        kk3 = kk.reshape(nch, A_CHUNK, LANES)
        k_neg = (kk3 * jnp.exp(-cum3)).astype(BF16)
        k_end = (kk3 * jnp.exp(end3 - cum3)).astype(BF16)
        v3 = v.reshape(nch, A_CHUNK, LANES).astype(BF16)
        decay = jnp.exp(end3)
        att = jnp.einsum("gcd,gsd->gcs", q_dec, k_neg, preferred_element_type=F32)
        keep = (si >= ci) if reverse else (si <= ci)
        att = jnp.where(keep, att, 0.0).astype(BF16)
        o_intra = jnp.einsum("gcs,gsv->gcv", att, v3, preferred_element_type=F32)
        outs = [None] * nch
        state = s_scr[d]
        for g in (range(nch - 1, -1, -1) if reverse else range(nch)):
            o_inter = lax.dot_general(q_dec[g], state.astype(BF16), (((1,), (1,)), ((), ())),
                                      preferred_element_type=F32)
            outs[g] = o_intra[g] + o_inter
            upd = lax.dot_general(v3[g], k_end[g], (((0,), (0,)), ((), ())),
                                  preferred_element_type=F32)
            state = decay[g] * state + upd
        s_scr[d] = state
        return jnp.concatenate(outs, axis=0), rows

    def body(t, carry):
        o_f, rows_f = direction(t * rt, ff_ref, lbf_ref, 0, False)
        o_ref[rows_f, :] = o_f
        o_b, rows_b = direction((n_tiles - 1 - t) * rt, fb_ref, lbb_ref, 1, True)
        ob_scr[rows_b, :] = o_b
        return carry

    lax.fori_loop(0, n_tiles, body, 0)
    o_ref[...] += ob_scr[...]
    st_ref[0, 0] = s_scr[...]


def gla_scan(z, lb_f, lb_b, s0, o_alias, *, batch, seg_len, row0):
    rows = z.shape[0]
    n_tiles = seg_len // GLA_TILE
    rb0 = row0 // seg_len
    assert row0 % seg_len == 0

    def col(kind):
        return pl.BlockSpec((seg_len, LANES), lambda b, h: (rb0 + b, kind * A_HEADS + h))

    lbspec = pl.BlockSpec((1, LANES), lambda b, h: (0, h))
    sspec = pl.BlockSpec((1, 1, 2, LANES, LANES), lambda b, h: (b, h, 0, 0, 0))
    kern = functools.partial(_gla_kernel, n_tiles=n_tiles)
    return pl.pallas_call(
        kern,
        out_shape=(jax.ShapeDtypeStruct((rows, A_HEADS * LANES), F32),
                   jax.ShapeDtypeStruct(s0.shape, F32)),
        grid=(batch, A_HEADS),
        in_specs=[col(0), col(1), col(2), col(3), lbspec, lbspec, sspec,
                  pl.BlockSpec(memory_space=pl.ANY)],
        out_specs=(pl.BlockSpec((seg_len, LANES), lambda b, h: (rb0 + b, h)), sspec),
        scratch_shapes=[pltpu.VMEM((2, LANES, LANES), F32), pltpu.VMEM((seg_len, LANES), F32)],
        input_output_aliases={7: 0},
        compiler_params=_cparams("parallel", "parallel"),
        name="gla_scan",
    )(z, z, z, z, lb_f.reshape(1, -1), lb_b.reshape(1, -1), s0, o_alias)


def _flash(q, kc, vc, kl_refs, vl_ref, n_kblocks):
    nt = (((1,), (1,)), ((), ()))
    m = l = acc = None
    for blk in range(n_kblocks + 1):
        if blk == 0:
            kb, vb = kc, vc
        else:
            ks = slice((blk - 1) * KEY_BLOCK, blk * KEY_BLOCK)
            parts = [r[ks, :].astype(BF16) for r in kl_refs]
            kb = parts[0] if len(parts) == 1 else jnp.concatenate(parts, axis=1)
            vb = vl_ref[ks, :]
        s = lax.dot_general(q, kb, nt, preferred_element_type=F32)
        m_blk = jnp.max(s, axis=-1, keepdims=True)
        if blk == 0:
            m = m_blk
            p = jnp.exp2(s - m)
            l = jnp.sum(p, axis=-1, keepdims=True)
            acc = jnp.dot(p.astype(BF16), vb, preferred_element_type=F32)
        else:
            m_new = jnp.maximum(m, m_blk)
            alpha = jnp.exp2(m - m_new)
            p = jnp.exp2(s - m_new)
            l = alpha * l + jnp.sum(p, axis=-1, keepdims=True)
            acc = alpha * acc + jnp.dot(p.astype(BF16), vb, preferred_element_type=F32)
            m = m_new
    return acc / l


def _attn_gqa_kernel(*refs, n_kblocks):
    if n_kblocks:
        q_ref, kc_ref, vc_ref, kl_ref, vl_ref, _, o_ref = refs
        kl = [kl_ref]
    else:
        q_ref, kc_ref, vc_ref, _, o_ref = refs
        kl, vl_ref = [], None
    group = C_HEADS // C_KV_HEADS
    tq = q_ref.shape[0]
    q = jnp.concatenate([q_ref[:, g * LANES:(g + 1) * LANES] for g in range(group)], axis=0)
    o = _flash(q, kc_ref[...], vc_ref[...], kl, vl_ref, n_kblocks)
    for g in range(group):
        o_ref[:, g * LANES:(g + 1) * LANES] = o[g * tq:(g + 1) * tq].astype(o_ref.dtype)


def _attn_mla_kernel(*refs, n_kblocks):
    if n_kblocks:
        qn_ref, qr_ref, knc_ref, krc_ref, vc_ref, knl_ref, krl_ref, vl_ref, _, o_ref = refs
        kl = [knl_ref, krl_ref]
    else:
        qn_ref, qr_ref, knc_ref, krc_ref, vc_ref, _, o_ref = refs
        kl, vl_ref = [], None
    kc = jnp.concatenate([knc_ref[...], krc_ref[...].astype(BF16)], axis=1)
    qn, qr = qn_ref[...], qr_ref[...]
    tq = qn.shape[0]
    lane = lax.broadcasted_iota(jnp.int32, qn.shape, 1)
    zero = jnp.zeros_like(qn)
    qs = []
    for hd in range(2):
        mn = (lane // B_NOPE) == hd
        mr = ((lane % (LANES // 2)) // (B_ROPE // 2)) == hd
        qs.append(jnp.concatenate([jnp.where(mn, qn, zero), jnp.where(mr, qr, zero)], axis=1))
    o = _flash(jnp.concatenate(qs, axis=0), kc, vc_ref[...], kl, vl_ref, n_kblocks)
    o_ref[...] = jnp.where(lane < B_V, o[:tq], o[tq:]).astype(o_ref.dtype)


def _attn_diff_kernel(*refs, n_kblocks, out_scale):
    if n_kblocks:
        q_ref, lam_ref, og_ref, kc_ref, vc_ref, kl_ref, vl_ref, _, o_ref = refs
        kl = [kl_ref]
    else:
        q_ref, lam_ref, og_ref, kc_ref, vc_ref, _, o_ref = refs
        kl, vl_ref = [], None
    q = q_ref[...]
    tq = q.shape[0]
    lane = lax.broadcasted_iota(jnp.int32, q.shape, 1)
    zero = jnp.zeros_like(q)
    qs = [jnp.where(((lane // (D_QK_DIM // 2)) % 2) == comp, q, zero) for comp in range(2)]
    maps = _flash(jnp.concatenate(qs, axis=0), kc_ref[...], vc_ref[...], kl, vl_ref, n_kblocks)
    o = maps[:tq] - lam_ref[...] * maps[tq:]
    o_ref[...] = (_rms(o, og_ref[...]) * out_scale).astype(o_ref.dtype)


def attention(kind, srcs, o_alias, *, batch, seq, ctx_len, queries, extra=(), out_scale=1.0):
    n_lat = batch * seq
    lat = queries == "lat"
    tq = min(Q_TILE[kind], seq if lat else ctx_len)
    nq = (seq if lat else ctx_len) // tq
    q0 = 0 if lat else n_lat // tq
    c0 = n_lat // ctx_len
    n_kblocks = seq // KEY_BLOCK if lat else 0

    def qspec(width, colfn):
        return pl.BlockSpec((tq, width), lambda b, g, i: (q0 + b * nq + i, colfn(g)))

    def cspec(colfn):
        return pl.BlockSpec((ctx_len, LANES), lambda b, g, i: (c0 + b, colfn(g)))

    def lspec(colfn):
        return pl.BlockSpec((seq, LANES), lambda b, g, i: (b, colfn(g)))

    if kind == "gqa":
        (qkv,) = srcs
        groups = C_KV_HEADS
        gw = (C_HEADS // C_KV_HEADS) * LANES
        kcol = lambda g: C_HEADS + g
        vcol = lambda g: C_HEADS + C_KV_HEADS + g
        in_specs = [qspec(gw, lambda g: g), cspec(kcol), cspec(vcol)]
        args = [qkv, qkv, qkv]
        if lat:
            in_specs += [lspec(kcol), lspec(vcol)]
            args += [qkv, qkv]
        kern = functools.partial(_attn_gqa_kernel, n_kblocks=n_kblocks)
        ow = gw
    elif kind == "mla":
        q_all, kv_all, cin = srcs
        groups = B_HEADS // 2
        kr_col = lambda g: B_Q_RANK // LANES
        in_specs = [qspec(LANES, lambda g: g), qspec(LANES, lambda g: groups + g),
                    cspec(lambda g: g), cspec(kr_col), cspec(lambda g: groups + g)]
        args = [q_all, q_all, kv_all, cin, kv_all]
        if lat:
            in_specs += [lspec(lambda g: g), lspec(kr_col), lspec(lambda g: groups + g)]
            args += [kv_all, cin, kv_all]
        kern = functools.partial(_attn_mla_kernel, n_kblocks=n_kblocks)
        ow = LANES
    else:
        (qkv,) = srcs
        lam, onorm_g = extra
        groups = D_HEADS
        kcol = lambda g: D_HEADS + g
        vcol = lambda g: 2 * D_HEADS + g
        vec = pl.BlockSpec((1, LANES), lambda b, g, i: (0, 0))
        in_specs = [qspec(LANES, lambda g: g), vec, vec, cspec(kcol), cspec(vcol)]
        args = [qkv, lam, onorm_g.reshape(1, LANES), qkv, qkv]
        if lat:
            in_specs += [lspec(kcol), lspec(vcol)]
            args += [qkv, qkv]
        kern = functools.partial(_attn_diff_kernel, n_kblocks=n_kblocks,
                                 out_scale=out_scale)
        ow = LANES
    in_specs.append(pl.BlockSpec(memory_space=pl.ANY))
    args.append(o_alias)
    return pl.pallas_call(
        kern,
        out_shape=jax.ShapeDtypeStruct(o_alias.shape, o_alias.dtype),
        grid=(batch, groups, nq),
        in_specs=in_specs,
        out_specs=pl.BlockSpec((tq, ow), lambda b, g, i: (q0 + b * nq + i, g)),
        input_output_aliases={len(args) - 1: 0},
        compiler_params=_cparams("parallel", "parallel", "parallel"),
        name=f"attn_{kind}_{queries}",
    )(*args)


def _rope_tables(seq, tile, n_pairs, lane_slots):
    t = np.arange(seq)
    pos = np.stack([t // GRID_W, t % GRID_W], axis=0).astype(np.float32)
    inv = jnp.asarray(ROPE_THETA, F32) ** (-jnp.arange(0, 2 * n_pairs, 2, dtype=F32) / (2 * n_pairs))
    ang = jnp.asarray(pos)[:, :, None] * inv[None, None, :]
    cos, sin = jnp.cos(ang), jnp.sin(ang)
    ccols, scols = [], []
    for slot in lane_slots:
        if slot is None:
            ccols.append(jnp.ones((seq,), F32))
            scols.append(jnp.zeros((seq,), F32))
        else:
            axis, fi, sign = slot
            ccols.append(cos[axis, :, fi])
            scols.append(sign * sin[axis, :, fi])
    ctab = jnp.concatenate([jnp.stack(ccols, axis=1), jnp.ones((tile, LANES), F32)], axis=0)
    stab = jnp.concatenate([jnp.stack(scols, axis=1), jnp.zeros((tile, LANES), F32)], axis=0)
    return ctab, stab


def _gqa_layout():
    q = C_HEAD_DIM // 4
    perm = np.concatenate([np.arange(0, q), np.arange(2 * q, 3 * q), np.arange(q, 2 * q),
                           np.arange(3 * q, 4 * q)])
    slots = [(axis, j, sign) for sign in (-1.0, 1.0) for axis in (0, 1) for j in range(q)]
    return perm, slots


def _diff_layout():
    e = D_QK_DIM // 4
    perm, slots = [], []
    for half, sign in ((0, -1.0), (1, 1.0)):
        for comp in range(2):
            for axis in range(2):
                perm.extend(comp * D_QK_DIM + axis * 2 * e + half * e + j for j in range(e))
                slots.extend((axis, j, sign) for j in range(e))
    return np.asarray(perm), slots


def _mla_rope_layout():
    e = B_ROPE // 4
    dims, slots = [], []
    for half, sign in ((0, -1.0), (1, 1.0)):
        for hd in range(2):
            for axis in range(2):
                dims.extend((hd, axis * 2 * e + half * e + j) for j in range(e))
                slots.extend((axis, j, sign) for j in range(e))
        dims.extend([None] * (2 * B_ROPE // 2))
        slots.extend([None] * (2 * B_ROPE // 2))
    return dims, slots


def _gather_cols(w, cols):
    idx = np.asarray([0 if c is None else c for c in cols])
    keep = np.asarray([c is not None for c in cols])
    out = jnp.take(w, jnp.asarray(idx), axis=1)
    return jnp.where(jnp.asarray(keep)[None, :], out, 0).astype(BF16)


def kernel(x, c, ctx, c_ctx, mod_w, mod_b, norm_g, a_w_in, a_lb, a_onorm_g, a_w_out, b_w_in, b_qnorm_g, b_kvnorm_g, b_w_qb, b_w_kvb, b_w_out, c_w_in, c_qnorm_g, c_knorm_g, c_w_out, d_w_in, d_lambda, d_onorm_g, d_w_out, ffn_w_gu, ffn_w_down, moe_router, moe_w_gu, moe_w_down):
    batch, seq, d = x.shape
    ctx_len = ctx.shape[1]
    depth = mod_w.shape[0]
    n_lat = batch * seq
    lay_all = Layout(batch, seq, ctx_len, ROW_TILE, with_ctx=True)
    lay_lat = Layout(batch, seq, ctx_len, ROW_TILE, with_ctx=False)
    lay_proj = Layout(batch, seq, ctx_len, PROJ_TILE, with_ctx=True)
    rows = lay_all.rows

    h = jnp.concatenate([x.reshape(n_lat, d), ctx.reshape(batch * ctx_len, d)], axis=0)

    n_rows = 8 * ((batch + 1 + 7) // 8)
    cond = jnp.zeros((n_rows, d), F32).at[0].set(c_ctx).at[1:batch + 1].set(c)
    mod_all = modvec(cond, mod_w, mod_b)

    lower_bounds = jnp.cumsum(jax.nn.softmax(a_lb.astype(F32), axis=1), axis=1)

    for i in range(depth):
        last = i == depth - 1
        kind, j = i % N_MIXERS, i // N_MIXERS
        modt = mod_all[i].reshape(n_rows * N_MOD, 1, d)
        lay_out = lay_lat if last else lay_all
        g0, g1, g2, g3 = norm_g[i, 0], norm_g[i, 1], norm_g[i, 2], norm_g[i, 3]

        if kind == 0:
            z = project(h, g0, a_w_in[j].astype(BF16), lay_proj, modt=modt, which=(0, 1),
                        tn=A_HEADS * LANES, name="gla_in")
            o = jnp.zeros((rows, A_HEADS * LANES), F32)
            s0 = jnp.zeros((batch, A_HEADS, 2, LANES, LANES), F32)
            o, s_ctx = gla_scan(z, lower_bounds[0, i], lower_bounds[1, i], s0, o,
                                batch=batch, seg_len=ctx_len, row0=n_lat)
            o, _ = gla_scan(z, lower_bounds[0, i], lower_bounds[1, i], s_ctx, o,
                            batch=batch, seg_len=seq, row0=0)
            h = out_project(o, a_w_out[j].astype(BF16), h, g1, modt, lay_out,
                            gla_gate=(z, a_onorm_g[j], 4), name="gla_out")
        else:
            if kind == 1:
                dims, slots = _mla_rope_layout()
                rope = _rope_tables(seq, lay_proj.tile,B_ROPE // 4, slots)
                w_in = b_w_in[j]
                kr_cols = [None if dd is None else B_Q_RANK + B_KV_RANK + dd[1] for dd in dims]
                w_cat = jnp.concatenate(
                    [w_in[:, :B_Q_RANK].astype(BF16), _gather_cols(w_in, kr_cols),
                     w_in[:, B_Q_RANK:B_Q_RANK + B_KV_RANK].astype(BF16)], axis=1)
                nq_blocks = B_Q_RANK // LANES
                nkv_blocks = B_KV_RANK // LANES
                blocks = ([(None, False, None)] * nq_blocks + [(None, True, None)]
                          + [(None, False, None)] * nkv_blocks)
                cin = project(h, g0, w_cat, lay_proj, modt=modt, which=(0, 1), blocks=blocks,
                              rope=rope, name="mla_in")
                hw = B_NOPE + B_ROPE
                qn_cols = [hd * hw + t for hd in range(B_HEADS) for t in range(B_NOPE)]
                qr_cols = [None if dd is None else (2 * p + dd[0]) * hw + B_NOPE + dd[1]
                           for p in range(B_HEADS // 2) for dd in dims]
                w_q = _gather_cols(b_w_qb[j], qn_cols + qr_cols)
                n_blk = B_HEADS * B_NOPE // LANES
                qmul = (B_NOPE + B_ROPE) ** -0.5 * LOG2E
                q_all = project(cin, b_qnorm_g[j], w_q, lay_proj, x_block=0,
                                blocks=([(None, False, qmul)] * n_blk
                                        + [(None, True, qmul)] * (B_HEADS // 2)),
                                rope=rope, out_dtype=BF16, name="mla_q")
                kw = B_NOPE + B_V
                kn_cols = [hd * kw + t for hd in range(B_HEADS) for t in range(B_NOPE)]
                v_cols = [hd * kw + B_NOPE + t for hd in range(B_HEADS) for t in range(B_V)]
                w_kv = _gather_cols(b_w_kvb[j], kn_cols + v_cols)
                kv_all = project(cin, b_kvnorm_g[j], w_kv, lay_proj,
                                 x_block=(B_Q_RANK + LANES) // B_KV_RANK, out_dtype=BF16,
                                 name="mla_kv")
                srcs, akind, extra, out_scale = (q_all, kv_all, cin), "mla", (), 1.0
                w_out = b_w_out[j]
            elif kind == 2:
                perm, slots = _gqa_layout()
                rope = _rope_tables(seq, lay_proj.tile,C_HEAD_DIM // 4, slots)
                nqk = C_HEADS + C_KV_HEADS
                cols = [hd * C_HEAD_DIM + p for hd in range(nqk) for p in perm]
                cols += list(range(nqk * C_HEAD_DIM, (nqk + C_KV_HEADS) * C_HEAD_DIM))
                w_cat = _gather_cols(c_w_in[j], cols)
                gains = jnp.zeros((8, LANES), F32).at[0].set(c_qnorm_g[j][perm]).at[1].set(
                    c_knorm_g[j][perm])
                qmul = C_HEAD_DIM ** -0.5 * LOG2E
                blocks = ([(0, True, qmul)] * C_HEADS + [(1, True, None)] * C_KV_HEADS
                          + [(None, False, None)] * C_KV_HEADS)
                qkv = project(h, g0, w_cat, lay_proj, modt=modt, which=(0, 1), blocks=blocks,
                              head_gain=gains, rope=rope, out_dtype=BF16, name="gqa_in")
                srcs, akind, extra, out_scale = (qkv,), "gqa", (), 1.0
                w_out = c_w_out[j]
            else:
                perm, slots = _diff_layout()
                rope = _rope_tables(seq, lay_proj.tile,D_QK_DIM // 4, slots)
                nqk = 2 * D_HEADS
                cols = [hd * 2 * D_QK_DIM + p for hd in range(nqk) for p in perm]
                cols += list(range(nqk * 2 * D_QK_DIM, (nqk + D_HEADS) * 2 * D_QK_DIM))
                w_cat = _gather_cols(d_w_in[j], cols)
                qmul = D_QK_DIM ** -0.5 * LOG2E
                blocks = ([(None, True, qmul)] * D_HEADS + [(None, True, None)] * D_HEADS
                          + [(None, False, None)] * D_HEADS)
                qkv = project(h, g0, w_cat, lay_proj, modt=modt, which=(0, 1), blocks=blocks,
                              rope=rope, out_dtype=BF16, name="diff_in")
                lambda_init = 0.8 - 0.6 * math.exp(-0.3 * i)
                lam = d_lambda[j].astype(F32)
                lam_full = (jnp.exp(jnp.sum(lam[0] * lam[1])) - jnp.exp(jnp.sum(lam[2] * lam[3]))
                            + lambda_init)
                extra = (jnp.full((1, LANES), lam_full, F32), d_onorm_g[j])
                srcs, akind, out_scale = (qkv,), "diff", 1.0 - lambda_init
                w_out = d_w_out[j]
            a = jnp.zeros((rows, d), BF16)
            a = attention(akind, srcs, a, batch=batch, seq=seq, ctx_len=ctx_len, queries="lat",
                          extra=extra, out_scale=out_scale)
            if not last:
                a = attention(akind, srcs, a, batch=batch, seq=seq, ctx_len=ctx_len,
                              queries="ctx", extra=extra, out_scale=out_scale)
            h = out_project(a, w_out.astype(BF16), h, g1, modt, lay_out, name=f"{akind}_out")

        if i % 2 == 0:
            h = ffn_residual(h, g2, g3, ffn_w_gu[i // 2].astype(BF16),
                             ffn_w_down[i // 2].astype(BF16), modt, lay_out)
        else:
            h = moe_residual(h, g2, g3, moe_router[i // 2], moe_w_gu[i // 2].astype(BF16),
                             moe_w_down[i // 2].astype(BF16), modt, lay_out)
    return h[:n_lat].reshape(batch, seq, d)
```

```python
import functools
import math
from typing import Any, NamedTuple

import jax
import jax.numpy as jnp
import numpy as np
from jax import lax
from jax.experimental import pallas as pl
from jax.experimental.pallas import tpu as pltpu

F32 = jnp.float32
BF16 = jnp.bfloat16

N_MIXERS = 4
N_MOD = 6
GRID_W = 64
ROPE_THETA = 10000.0
NORM_EPS = 1e-6
A_HEADS = 8
A_DK = 128
A_CHUNK = 32
B_HEADS = 16
B_Q_RANK = 384
B_KV_RANK = 256
B_NOPE = 64
B_ROPE = 32
B_V = 64
C_HEADS = 8
C_KV_HEADS = 2
C_HEAD_DIM = 128
D_HEADS = 8
D_QK_DIM = 64
N_EXPERTS = 8
TOP_K = 2

LANES = 128
V7X_VMEM_BYTES = 64 * 1024 * 1024
VMEM_LIMIT = V7X_VMEM_BYTES - 8 * 1024 * 1024

ROW_TILE = 1024
PROJ_TILE = 512
Q_TILE = {"gqa": 256, "mla": 512, "diff": 512}
KEY_BLOCK = 512
LOG2E = 1.4426950408889634
GLA_TILE = 256
FF_CHUNK = 896
MOE_TILE = 512


def _cparams(*sem):
    return pltpu.CompilerParams(dimension_semantics=sem, vmem_limit_bytes=VMEM_LIMIT)


def _rms(x, g):
    return x * lax.rsqrt(jnp.mean(x * x, axis=-1, keepdims=True) + NORM_EPS) * g


def _silu(x):
    return x * jax.nn.sigmoid(x)


class Layout:
    def __init__(self, batch, seq, ctx_len, tile_cap, with_ctx=True):
        tile = math.gcd(math.gcd(seq, batch * ctx_len), tile_cap)
        self.batch, self.seq, self.ctx_len, self.tile = batch, seq, ctx_len, tile
        self.n_lat = batch * seq
        self.n_ctx = batch * ctx_len if with_ctx else 0
        self.rows = self.n_lat + self.n_ctx
        assert seq % tile == 0 and self.n_ctx % tile == 0
        self.lat_tiles = self.n_lat // tile
        self.tiles = self.rows // tile
        self.tiles_per_batch = seq // tile

    def mod_row(self, i):
        return jnp.where(i < self.lat_tiles, 1 + i // self.tiles_per_batch, 0)

    def rope_block(self, i):
        return jnp.where(i < self.lat_tiles, i % self.tiles_per_batch, self.tiles_per_batch)

    def mod_spec(self, which, d):
        return pl.BlockSpec((1, 1, d), lambda i, *_: (self.mod_row(i) * N_MOD + which, 0, 0))


def _modvec_kernel(c_ref, w_ref, b_ref, o_ref):
    cond = _silu(c_ref[...]).astype(BF16)
    o_ref[0] = jnp.dot(cond, w_ref[0].astype(BF16), preferred_element_type=F32) + b_ref[0]


def modvec(cond_rows, mod_w, mod_b):
    depth, d, n = mod_w.shape
    r = cond_rows.shape[0]
    tn = d
    return pl.pallas_call(
        _modvec_kernel,
        out_shape=jax.ShapeDtypeStruct((depth, r, n), F32),
        grid=(depth, n // tn),
        in_specs=[pl.BlockSpec((r, d), lambda l, j: (0, 0)),
                  pl.BlockSpec((1, d, tn), lambda l, j: (l, 0, j)),
                  pl.BlockSpec((1, 1, tn), lambda l, j: (l, 0, j))],
        out_specs=pl.BlockSpec((1, r, tn), lambda l, j: (l, 0, j)),
        compiler_params=_cparams("parallel", "parallel"),
        name="modvec",
    )(cond_rows, mod_w, mod_b.reshape(depth, 1, n))


class Blk(NamedTuple):
    src: Any
    gain: Any = None
    rope: bool = False
    mult: Any = None


def _proj_kernel(*refs, has_mod, blocks, vt_srcs, has_gain, has_rope, has_copy):
    it = iter(refs)
    x_ref, g_ref = next(it), next(it)
    sh_ref = sc_ref = hg_ref = c_ref = s_ref = cp_ref = vt_ref = None
    if has_mod:
        sh_ref, sc_ref = next(it), next(it)
    w_ref = next(it)
    if has_gain:
        hg_ref = next(it)
    if has_rope:
        c_ref, s_ref = next(it), next(it)
    if has_copy:
        cp_ref = next(it)
    o_ref = next(it)
    if vt_srcs:
        vt_ref = next(it)
    u_ref = next(it)

    @pl.when(pl.program_id(1) == 0)
    def _():
        y = _rms(x_ref[...].astype(F32), g_ref[...])
        if has_mod:
            y = y * (1 + sc_ref[0]) + sh_ref[0]
        u_ref[...] = y.astype(BF16)

    acc = jnp.dot(u_ref[...], w_ref[...], preferred_element_type=F32)
    if blocks is None:
        o_ref[...] = acc.astype(o_ref.dtype)
        return
    for c, blk in enumerate(blocks):
        if blk.src == "copy":
            a = cp_ref[...].astype(F32)
        else:
            a = acc[:, blk.src * LANES:(blk.src + 1) * LANES]
        if blk.gain is not None:
            a = _rms(a, hg_ref[blk.gain:blk.gain + 1, :])
        if blk.rope:
            a = a * c_ref[...] + pltpu.roll(a, LANES // 2, 1) * s_ref[...]
        if blk.mult is not None:
            a = a * blk.mult
        o_ref[:, c * LANES:(c + 1) * LANES] = a.astype(o_ref.dtype)
    for c, src in enumerate(vt_srcs):
        a = acc[:, src * LANES:(src + 1) * LANES]
        vt_ref[c * LANES:(c + 1) * LANES, :] = a.T.astype(vt_ref.dtype)


def project(x, g, w, lay, *, modt=None, which=None, x_block=0, tn=None, blocks=None, vt_srcs=(),
            head_gain=None, rope=None, copy_src=None, out_dtype=F32, name="project"):
    k, n = w.shape
    tm = lay.tile
    tn = n if tn is None else tn
    assert blocks is None or tn == n
    n_out = n if blocks is None else len(blocks) * LANES
    in_specs = [pl.BlockSpec((tm, k), lambda i, j: (i, x_block)),
                pl.BlockSpec((1, k), lambda i, j: (0, 0))]
    args = [x, g.reshape(1, k)]
    if modt is not None:
        in_specs += [lay.mod_spec(which[0], k), lay.mod_spec(which[1], k)]
        args += [modt, modt]
    in_specs.append(pl.BlockSpec((k, tn), lambda i, j: (0, j)))
    args.append(w)
    if head_gain is not None:
        in_specs.append(pl.BlockSpec(head_gain.shape, lambda i, j: (0, 0)))
        args.append(head_gain)
    if rope is not None:
        spec = pl.BlockSpec((tm, LANES), lambda i, j: (lay.rope_block(i), 0))
        in_specs += [spec, spec]
        args += list(rope)
    if copy_src is not None:
        cp_arr, cp_block = copy_src
        in_specs.append(pl.BlockSpec((tm, LANES), lambda i, j: (i, cp_block)))
        args.append(cp_arr)
    out_shape = [jax.ShapeDtypeStruct((lay.rows, n_out), out_dtype)]
    out_specs = [pl.BlockSpec((tm, n_out if blocks is not None else tn), lambda i, j: (i, j))]
    if vt_srcs:
        out_shape.append(jax.ShapeDtypeStruct((len(vt_srcs) * LANES, lay.rows), BF16))
        out_specs.append(pl.BlockSpec((len(vt_srcs) * LANES, tm), lambda i, j: (0, i)))
    kern = functools.partial(_proj_kernel, has_mod=modt is not None, blocks=blocks,
                             vt_srcs=tuple(vt_srcs), has_gain=head_gain is not None,
                             has_rope=rope is not None, has_copy=copy_src is not None)
    outs = pl.pallas_call(
        kern,
        out_shape=out_shape,
        grid=(lay.tiles, n // tn),
        in_specs=in_specs,
        out_specs=out_specs,
        scratch_shapes=[pltpu.VMEM((tm, k), BF16)],
        compiler_params=_cparams("parallel", "arbitrary"),
        name=name,
    )(*args)
    return outs if vt_srcs else outs[0]


def _oproj_kernel(a_ref, w_ref, h_ref, g_ref, gate_ref, o_ref):
    y = jnp.dot(a_ref[...], w_ref[...], preferred_element_type=F32)
    o_ref[...] = h_ref[...] + gate_ref[0] * _rms(y, g_ref[...])


def _oproj_gla_kernel(o_in_ref, gz_ref, og_ref, w_ref, h_ref, g_ref, gate_ref, o_ref):
    o = o_in_ref[...]
    gz = gz_ref[...]
    parts = []
    for hd in range(A_HEADS):
        sl = slice(hd * LANES, (hd + 1) * LANES)
        parts.append((_rms(o[:, sl], og_ref[...]) * _silu(gz[:, sl])).astype(BF16))
    a = jnp.concatenate(parts, axis=1)
    y = jnp.dot(a, w_ref[...], preferred_element_type=F32)
    o_ref[...] = h_ref[...] + gate_ref[0] * _rms(y, g_ref[...])


def out_project(a, w, h, g, modt, lay, *, gla_gate=None, name="out_project"):
    k, d = w.shape
    tm = lay.tile
    row = lambda i: (i, 0)
    if gla_gate is None:
        kern = _oproj_kernel
        in_specs = [pl.BlockSpec((tm, k), row)]
        args = [a]
    else:
        z, onorm_g, gate_block = gla_gate
        kern = _oproj_gla_kernel
        in_specs = [pl.BlockSpec((tm, k), row),
                    pl.BlockSpec((tm, k), lambda i: (i, gate_block)),
                    pl.BlockSpec((1, LANES), lambda i: (0, 0))]
        args = [a, z, onorm_g.reshape(1, LANES)]
    in_specs += [pl.BlockSpec((k, d), lambda i: (0, 0)),
                 pl.BlockSpec((tm, d), row),
                 pl.BlockSpec((1, d), lambda i: (0, 0)),
                 lay.mod_spec(2, d)]
    args += [w, h, g.reshape(1, d), modt]
    return pl.pallas_call(
        kern,
        out_shape=jax.ShapeDtypeStruct((lay.rows, d), F32),
        grid=(lay.tiles,),
        in_specs=in_specs,
        out_specs=pl.BlockSpec((tm, d), row),
        compiler_params=_cparams("parallel"),
        name=name,
    )(*args)


def _ffn_kernel(h_ref, g2_ref, sh_ref, sc_ref, wg_ref, wu_ref, wd_ref, g3_ref, gate_ref, o_ref,
                u_ref, acc_ref):
    j = pl.program_id(1)

    @pl.when(j == 0)
    def _():
        y = _rms(h_ref[...], g2_ref[...]) * (1 + sc_ref[0]) + sh_ref[0]
        u_ref[...] = y.astype(BF16)
        acc_ref[...] = jnp.zeros_like(acc_ref)

    u = u_ref[...]
    a = jnp.dot(u, wg_ref[...], preferred_element_type=F32)
    b = jnp.dot(u, wu_ref[...], preferred_element_type=F32)
    hid = (_silu(a) * b).astype(BF16)
    acc_ref[...] += jnp.dot(hid, wd_ref[...], preferred_element_type=F32)

    @pl.when(j == pl.num_programs(1) - 1)
    def _():
        o_ref[...] = h_ref[...] + gate_ref[0] * _rms(acc_ref[...], g3_ref[...])


def ffn_residual(h, g2, g3, w_gu, w_down, modt, lay):
    d = h.shape[1]
    ff = w_down.shape[0]
    fc = FF_CHUNK
    nfc = ff // fc
    tm = lay.tile
    row = lambda i, j: (i, 0)
    vec = pl.BlockSpec((1, d), lambda i, j: (0, 0))
    return pl.pallas_call(
        _ffn_kernel,
        out_shape=jax.ShapeDtypeStruct((lay.rows, d), F32),
        grid=(lay.tiles, nfc),
        in_specs=[pl.BlockSpec((tm, d), row), vec, lay.mod_spec(3, d), lay.mod_spec(4, d),
                  pl.BlockSpec((d, fc), lambda i, j: (0, j)),
                  pl.BlockSpec((d, fc), lambda i, j: (0, j + nfc)),
                  pl.BlockSpec((fc, d), lambda i, j: (j, 0)),
                  vec, lay.mod_spec(5, d)],
        out_specs=pl.BlockSpec((tm, d), row),
        scratch_shapes=[pltpu.VMEM((tm, d), BF16), pltpu.VMEM((tm, d), F32)],
        compiler_params=_cparams("parallel", "arbitrary"),
        name="ffn_residual",
    )(h, g2.reshape(1, d), modt, modt, w_gu, w_gu, w_down, g3.reshape(1, d), modt)


def _router_kernel(h_ref, g2_ref, sh_ref, sc_ref, r_ref, u_ref, ids_ref, wts_ref):
    u = (_rms(h_ref[...], g2_ref[...]) * (1 + sc_ref[0]) + sh_ref[0]).astype(BF16)
    u_ref[...] = u
    logits = jnp.dot(u, r_ref[...], preferred_element_type=F32)
    lane = lax.broadcasted_iota(jnp.int32, logits.shape, 1)
    neg = jnp.float32(-jnp.inf)
    logits = jnp.where(lane < N_EXPERTS, logits, neg)
    m1 = jnp.max(logits, axis=-1, keepdims=True)
    i1 = jnp.min(jnp.where(logits == m1, lane, LANES), axis=-1, keepdims=True)
    rest = jnp.where(lane == i1, neg, logits)
    m2 = jnp.max(rest, axis=-1, keepdims=True)
    i2 = jnp.min(jnp.where(rest == m2, lane, LANES), axis=-1, keepdims=True)
    e = jnp.exp(m2 - m1)
    w1 = 1.0 / (1.0 + e)
    w2 = e / (1.0 + e)
    ids_ref[...] = jnp.where(lane == 0, i1, jnp.where(lane == 1, i2, 0))
    wts_ref[...] = jnp.where(lane == 0, w1, jnp.where(lane == 1, w2, 0.0))


def moe_router(h, g2, router, modt, lay):
    d = h.shape[1]
    tm = lay.tile
    r_pad = jnp.zeros((d, LANES), BF16).at[:, :N_EXPERTS].set(router.astype(BF16))
    row = lambda i: (i, 0)
    return pl.pallas_call(
        _router_kernel,
        out_shape=(jax.ShapeDtypeStruct((lay.rows, d), BF16),
                   jax.ShapeDtypeStruct((lay.rows, LANES), jnp.int32),
                   jax.ShapeDtypeStruct((lay.rows, LANES), F32)),
        grid=(lay.tiles,),
        in_specs=[pl.BlockSpec((tm, d), row), pl.BlockSpec((1, d), lambda i: (0, 0)),
                  lay.mod_spec(3, d), lay.mod_spec(4, d),
                  pl.BlockSpec((d, LANES), lambda i: (0, 0))],
        out_specs=(pl.BlockSpec((tm, d), row), pl.BlockSpec((tm, LANES), row),
                   pl.BlockSpec((tm, LANES), row)),
        compiler_params=_cparams("parallel"),
        name="moe_router",
    )(h, g2.reshape(1, d), modt, modt, r_pad)


def _moe_ffn_kernel(te_ref, nu_ref, x_ref, wg_ref, wu_ref, wd_ref, o_ref, acc_ref):
    t, j = pl.program_id(0), pl.program_id(1)

    @pl.when(j == 0)
    def _():
        acc_ref[...] = jnp.zeros_like(acc_ref)

    @pl.when(t < nu_ref[0])
    def _():
        x = x_ref[...]
        a = jnp.dot(x, wg_ref[0], preferred_element_type=F32)
        b = jnp.dot(x, wu_ref[0], preferred_element_type=F32)
        hid = (_silu(a) * b).astype(BF16)
        acc_ref[...] += jnp.dot(hid, wd_ref[0], preferred_element_type=F32)

    @pl.when(j == pl.num_programs(1) - 1)
    def _():
        o_ref[...] = acc_ref[...]


def moe_ffn(x_sorted, tile_expert, n_used, w_gu, w_down):
    n_slots, d = x_sorted.shape
    ff = w_down.shape[1]
    fc = FF_CHUNK
    nfc = ff // fc
    tm = MOE_TILE
    n_tiles = n_slots // tm

    def jj(t, j, te, nu):
        return jnp.where(t < nu[0], j, nfc - 1)

    grid_spec = pltpu.PrefetchScalarGridSpec(
        num_scalar_prefetch=2,
        grid=(n_tiles, nfc),
        in_specs=[pl.BlockSpec((tm, d), lambda t, j, te, nu: (t, 0)),
                  pl.BlockSpec((1, d, fc), lambda t, j, te, nu: (te[t], 0, jj(t, j, te, nu))),
                  pl.BlockSpec((1, d, fc), lambda t, j, te, nu: (te[t], 0, jj(t, j, te, nu) + nfc)),
                  pl.BlockSpec((1, fc, d), lambda t, j, te, nu: (te[t], jj(t, j, te, nu), 0))],
        out_specs=pl.BlockSpec((tm, d), lambda t, j, te, nu: (t, 0)),
        scratch_shapes=[pltpu.VMEM((tm, d), F32)],
    )
    return pl.pallas_call(
        _moe_ffn_kernel,
        out_shape=jax.ShapeDtypeStruct((n_slots, d), F32),
        grid_spec=grid_spec,
        compiler_params=_cparams("arbitrary", "arbitrary"),
        name="moe_ffn",
    )(tile_expert, n_used, x_sorted, w_gu, w_gu, w_down)


def _combine_kernel(y_ref, w_ref, h_ref, g_ref, gate_ref, o_ref):
    w = w_ref[...]
    y = w[:, 0:1] * y_ref[0] + w[:, 1:2] * y_ref[1]
    o_ref[...] = h_ref[...] + gate_ref[0] * _rms(y, g_ref[...])


def combine_residual(y_pair, wts, h, g3, modt, lay):
    d = h.shape[1]
    tm = lay.tile
    row = lambda i: (i, 0)
    return pl.pallas_call(
        _combine_kernel,
        out_shape=jax.ShapeDtypeStruct((lay.rows, d), F32),
        grid=(lay.tiles,),
        in_specs=[pl.BlockSpec((2, tm, d), lambda i: (0, i, 0)), pl.BlockSpec((tm, LANES), row),
                  pl.BlockSpec((tm, d), row), pl.BlockSpec((1, d), lambda i: (0, 0)),
                  lay.mod_spec(5, d)],
        out_specs=pl.BlockSpec((tm, d), row),
        compiler_params=_cparams("parallel"),
        name="moe_combine",
    )(y_pair, wts, h, g3.reshape(1, d), modt)


def moe_residual(h, g2, g3, router, w_gu, w_down, modt, lay):
    rows = lay.rows
    u, ids, wts = moe_router(h, g2, router, modt, lay)
    e_flat = ids[:, :TOP_K].reshape(-1)
    n = rows * TOP_K
    onehot = (e_flat[:, None] == jnp.arange(N_EXPERTS, dtype=jnp.int32)[None, :]).astype(jnp.int32)
    csum = jnp.cumsum(onehot, axis=0)
    counts = csum[-1]
    rank = jnp.sum(csum * onehot, axis=1) - 1
    padded = ((counts + MOE_TILE - 1) // MOE_TILE) * MOE_TILE
    ends = jnp.cumsum(padded)
    starts = ends - padded
    dest = jnp.sum(starts[None, :] * onehot, axis=1) + rank
    n_slots = ((n + N_EXPERTS * (MOE_TILE - 1)) // MOE_TILE + 1) * MOE_TILE
    n_tiles = n_slots // MOE_TILE
    src_tok = jnp.zeros((n_slots,), jnp.int32).at[dest].set(
        jnp.arange(n, dtype=jnp.int32) // TOP_K, unique_indices=True)
    n_used = (ends[-1] // MOE_TILE).astype(jnp.int32).reshape(1)
    tile_start = jnp.arange(n_tiles, dtype=jnp.int32) * MOE_TILE
    tile_expert = jnp.minimum(
        jnp.sum((ends[None, :] <= tile_start[:, None]).astype(jnp.int32), axis=1), N_EXPERTS - 1)
    last_e = jnp.max(jnp.where(counts > 0, jnp.arange(N_EXPERTS, dtype=jnp.int32), 0))
    tile_expert = jnp.where(tile_start < ends[-1], tile_expert, last_e)
    x_sorted = jnp.take(u, src_tok, axis=0)
    y_sorted = moe_ffn(x_sorted, tile_expert, n_used, w_gu, w_down)
    y_pair = jnp.take(y_sorted, dest.reshape(rows, TOP_K).T, axis=0)
    return combine_residual(y_pair, wts, h, g3, modt, lay)


def _gla_kernel(*refs, n_tiles, has_alias):
    q_ref, ff_ref, fb_ref, v_ref, lbf_ref, lbb_ref, s0_ref = refs[:7]
    o_ref, st_ref, s_scr, ob_scr = refs[8:] if has_alias else refs[7:]
    rt = GLA_TILE
    nch = rt // A_CHUNK
    s_scr[...] = s0_ref[0, 0]
    row = lax.broadcasted_iota(jnp.int32, (rt, LANES), 0) % A_CHUNK
    ci = lax.broadcasted_iota(jnp.int32, (rt, rt), 0)
    si = lax.broadcasted_iota(jnp.int32, (rt, rt), 1)
    same_chunk = (ci // A_CHUNK) == (si // A_CHUNK)
    nt = (((1,), (1,)), ((), ()))

    def direction(r0, f_ref, lb_ref, d, reverse):
        rows = pl.ds(pl.multiple_of(r0, rt), rt)
        lb = lb_ref[...]
        f = lb + (1.0 - lb) * jax.nn.sigmoid(f_ref[rows, :])
        logf = jnp.log(f)
        kk = 1.0 - f
        q = q_ref[rows, :] * (A_DK ** -0.5)
        v = v_ref[rows, :]
        cum = logf
        for s in (1, 2, 4, 8, 16):
            if reverse:
                cum = cum + jnp.where(row < A_CHUNK - s, pltpu.roll(cum, rt - s, 0), 0.0)
            else:
                cum = cum + jnp.where(row >= s, pltpu.roll(cum, s, 0), 0.0)
        cum3 = cum.reshape(nch, A_CHUNK, LANES)
        end3 = cum3[:, 0:1, :] if reverse else cum3[:, A_CHUNK - 1:A_CHUNK, :]
        q_dec = (q * jnp.exp(cum)).astype(BF16)
        k_neg = (kk * jnp.exp(-cum)).astype(BF16)
        k_end = (kk.reshape(nch, A_CHUNK, LANES) * jnp.exp(end3 - cum3)).astype(BF16)
        vb = v.astype(BF16)
        decay = jnp.exp(end3)
        att = lax.dot_general(q_dec, k_neg, nt, preferred_element_type=F32)
        keep = same_chunk & ((si >= ci) if reverse else (si <= ci))
        o_intra = jnp.dot(jnp.where(keep, att, 0.0).astype(BF16), vb, preferred_element_type=F32)
        q_dec3 = q_dec.reshape(nch, A_CHUNK, LANES)
        v3 = vb.reshape(nch, A_CHUNK, LANES)
        upds = [lax.dot_general(v3[g], k_end[g], (((0,), (0,)), ((), ())),
                                preferred_element_type=F32) for g in range(nch)]
        before = [None] * nch
        state = s_scr[d]
        for g in (range(nch - 1, -1, -1) if reverse else range(nch)):
            before[g] = state.astype(BF16)
            state = decay[g] * state + upds[g]
        s_scr[d] = state
        outs = [lax.dot_general(q_dec3[g], before[g], nt, preferred_element_type=F32)
                for g in range(nch)]
        return o_intra + jnp.concatenate(outs, axis=0), rows

    def body(t, carry):
        o_f, rows_f = direction(t * rt, ff_ref, lbf_ref, 0, False)
        o_ref[rows_f, :] = o_f
        o_b, rows_b = direction((n_tiles - 1 - t) * rt, fb_ref, lbb_ref, 1, True)
        ob_scr[rows_b, :] = o_b
        return carry

    lax.fori_loop(0, n_tiles, body, 0)
    o_ref[...] += ob_scr[...]
    st_ref[0, 0] = s_scr[...]


def gla_scan(z, lb_f, lb_b, s0, o_alias, *, batch, seg_len, row0):
    rows = z.shape[0]
    n_tiles = seg_len // GLA_TILE
    rb0 = row0 // seg_len
    assert row0 % seg_len == 0

    def col(kind):
        return pl.BlockSpec((seg_len, LANES), lambda b, h: (rb0 + b, kind * A_HEADS + h))

    lbspec = pl.BlockSpec((1, LANES), lambda b, h: (0, h))
    sspec = pl.BlockSpec((1, 1, 2, LANES, LANES), lambda b, h: (b, h, 0, 0, 0))
    has_alias = o_alias is not None
    kern = functools.partial(_gla_kernel, n_tiles=n_tiles, has_alias=has_alias)
    in_specs = [col(0), col(1), col(2), col(3), lbspec, lbspec, sspec]
    args = [z, z, z, z, lb_f.reshape(1, -1), lb_b.reshape(1, -1), s0]
    if has_alias:
        in_specs.append(pl.BlockSpec(memory_space=pl.ANY))
        args.append(o_alias)
    return pl.pallas_call(
        kern,
        out_shape=(jax.ShapeDtypeStruct((rows, A_HEADS * LANES), F32),
                   jax.ShapeDtypeStruct(s0.shape, F32)),
        grid=(batch, A_HEADS),
        in_specs=in_specs,
        out_specs=(pl.BlockSpec((seg_len, LANES), lambda b, h: (rb0 + b, h)), sspec),
        scratch_shapes=[pltpu.VMEM((2, LANES, LANES), F32), pltpu.VMEM((seg_len, LANES), F32)],
        input_output_aliases={7: 0} if has_alias else {},
        compiler_params=_cparams("parallel", "parallel"),
        name="gla_scan",
    )(*args)


def _flash(q, kc, vc, kl_ref, vl_ref):
    nt = (((1,), (1,)), ((), ()))
    n_kblocks = 0 if kl_ref is None else kl_ref.shape[0] // KEY_BLOCK
    m = l = acc = None
    for blk in range(n_kblocks + 1):
        if blk == 0:
            kb, vb = kc, vc
        else:
            ks = slice((blk - 1) * KEY_BLOCK, blk * KEY_BLOCK)
            kb, vb = kl_ref[ks, :], vl_ref[ks, :]
        s = lax.dot_general(q, kb, nt, preferred_element_type=F32)
        m_blk = jnp.max(s, axis=-1, keepdims=True)
        m_new = m_blk if blk == 0 else jnp.maximum(m, m_blk)
        p = jnp.exp2(s - m_new)
        psum = p[:, :LANES]
        for t in range(1, p.shape[1] // LANES):
            psum = psum + p[:, t * LANES:(t + 1) * LANES]
        pv = jnp.dot(p.astype(BF16), vb, preferred_element_type=F32)
        if blk == 0:
            l, acc = psum, pv
        else:
            alpha = jnp.exp2(m - m_new)
            l = alpha * l + psum
            acc = alpha * acc + pv
        m = m_new
    return acc / jnp.sum(l, axis=-1, keepdims=True)


def _split_attn_refs(refs, n_lead, has_lat):
    lead = refs[:n_lead]
    kc_ref, vc_ref = refs[n_lead], refs[n_lead + 1]
    if has_lat:
        kl_ref, vl_ref, o_ref = refs[n_lead + 2:]
    else:
        kl_ref = vl_ref = None
        _, o_ref = refs[n_lead + 2:]
    return lead, kc_ref, vc_ref, kl_ref, vl_ref, o_ref


def _attn_gqa_kernel(*refs, has_lat):
    (q_ref,), kc_ref, vc_ref, kl_ref, vl_ref, o_ref = _split_attn_refs(refs, 1, has_lat)
    group = C_HEADS // C_KV_HEADS
    tq = q_ref.shape[0]
    q = jnp.concatenate([q_ref[:, g * LANES:(g + 1) * LANES] for g in range(group)], axis=0)
    o = _flash(q, kc_ref[...], vc_ref[...], kl_ref, vl_ref)
    for g in range(group):
        o_ref[:, g * LANES:(g + 1) * LANES] = o[g * tq:(g + 1) * tq].astype(o_ref.dtype)


def _attn_mla_kernel(*refs, has_lat):
    (qn_ref, qr_ref), kc_ref, vc_ref, kl_ref, vl_ref, o_ref = _split_attn_refs(refs, 2, has_lat)
    qn, qr = qn_ref[...], qr_ref[...]
    tq = qn.shape[0]
    lane = lax.broadcasted_iota(jnp.int32, qn.shape, 1)
    zero = jnp.zeros_like(qn)
    qs = []
    for hd in range(2):
        mn = (lane // B_NOPE) == hd
        mr = ((lane % (LANES // 2)) // (B_ROPE // 2)) == hd
        qs.append(jnp.concatenate([jnp.where(mn, qn, zero), jnp.where(mr, qr, zero)], axis=1))
    o = _flash(jnp.concatenate(qs, axis=0), kc_ref[...], vc_ref[...], kl_ref, vl_ref)
    o_ref[...] = jnp.where(lane < B_V, o[:tq], o[tq:]).astype(o_ref.dtype)


def _attn_diff_kernel(*refs, has_lat, out_scale):
    (q_ref, lam_ref, og_ref), kc_ref, vc_ref, kl_ref, vl_ref, o_ref = _split_attn_refs(
        refs, 3, has_lat)
    q = q_ref[...]
    tq = q.shape[0]
    lane = lax.broadcasted_iota(jnp.int32, q.shape, 1)
    zero = jnp.zeros_like(q)
    qs = [jnp.where(((lane // (D_QK_DIM // 2)) % 2) == comp, q, zero) for comp in range(2)]
    maps = _flash(jnp.concatenate(qs, axis=0), kc_ref[...], vc_ref[...], kl_ref, vl_ref)
    o = maps[:tq] - lam_ref[...] * maps[tq:]
    o_ref[...] = (_rms(o, og_ref[...]) * out_scale).astype(o_ref.dtype)


def attention(kind, srcs, o_alias, *, batch, seq, ctx_len, queries, extra=(), out_scale=1.0):
    n_lat = batch * seq
    lat = queries == "lat"
    tq = min(Q_TILE[kind], seq if lat else ctx_len)
    nq = (seq if lat else ctx_len) // tq
    q0 = 0 if lat else n_lat // tq
    c0 = n_lat // ctx_len
    rows = n_lat + batch * ctx_len

    def qspec(width, colfn):
        return pl.BlockSpec((tq, width), lambda b, g, i: (q0 + b * nq + i, colfn(g)))

    def kvspecs(kwidth, kcol, vcol):
        specs = [pl.BlockSpec((ctx_len, kwidth), lambda b, g, i: (c0 + b, kcol(g))),
                 pl.BlockSpec((ctx_len, LANES), lambda b, g, i: (c0 + b, vcol(g)))]
        if lat:
            specs += [pl.BlockSpec((seq, kwidth), lambda b, g, i: (b, kcol(g))),
                      pl.BlockSpec((seq, LANES), lambda b, g, i: (b, vcol(g)))]
        return specs

    n_kv = 4 if lat else 2
    if kind == "gqa":
        (qkv,) = srcs
        groups = C_KV_HEADS
        ow = (C_HEADS // C_KV_HEADS) * LANES
        in_specs = [qspec(ow, lambda g: g)] + kvspecs(
            LANES, lambda g: C_HEADS + g, lambda g: C_HEADS + C_KV_HEADS + g)
        args = [qkv] * (1 + n_kv)
        kern = functools.partial(_attn_gqa_kernel, has_lat=lat)
    elif kind == "mla":
        q_all, kv_all = srcs
        groups = B_HEADS // 2
        ow = LANES
        in_specs = ([qspec(LANES, lambda g: g), qspec(LANES, lambda g: groups + g)]
                    + kvspecs(2 * LANES, lambda g: g, lambda g: 2 * groups + g))
        args = [q_all, q_all] + [kv_all] * n_kv
        kern = functools.partial(_attn_mla_kernel, has_lat=lat)
    else:
        (qkv,) = srcs
        lam, onorm_g = extra
        groups = D_HEADS
        ow = LANES
        vec = pl.BlockSpec((1, LANES), lambda b, g, i: (0, 0))
        in_specs = [qspec(LANES, lambda g: g), vec, vec] + kvspecs(
            LANES, lambda g: D_HEADS + g, lambda g: 2 * D_HEADS + g)
        args = [qkv, lam, onorm_g.reshape(1, LANES)] + [qkv] * n_kv
        kern = functools.partial(_attn_diff_kernel, has_lat=lat, out_scale=out_scale)
    aliases = {}
    if not lat:
        in_specs.append(pl.BlockSpec(memory_space=pl.ANY))
        args.append(o_alias)
        aliases = {len(args) - 1: 0}
    return pl.pallas_call(
        kern,
        out_shape=jax.ShapeDtypeStruct((rows, groups * ow), BF16),
        grid=(batch, groups, nq),
        in_specs=in_specs,
        out_specs=pl.BlockSpec((tq, ow), lambda b, g, i: (q0 + b * nq + i, g)),
        input_output_aliases=aliases,
        compiler_params=_cparams("parallel", "parallel", "parallel"),
        name=f"attn_{kind}_{queries}",
    )(*args)


def _rope_tables(seq, tile, n_pairs, lane_slots):
    axis = np.asarray([0 if s is None else s[0] for s in lane_slots])
    freq = np.asarray([0 if s is None else s[1] for s in lane_slots])
    sign = np.asarray([0.0 if s is None else s[2] for s in lane_slots], np.float32)
    live = np.asarray([s is not None for s in lane_slots])
    t = jnp.arange(seq, dtype=jnp.int32)
    pos = jnp.where(jnp.asarray(axis)[None, :] == 0, (t // GRID_W)[:, None], (t % GRID_W)[:, None])
    inv = jnp.asarray(ROPE_THETA, F32) ** (-jnp.arange(0, 2 * n_pairs, 2, dtype=F32) / (2 * n_pairs))
    ang = pos.astype(F32) * inv[jnp.asarray(freq)][None, :]
    cos = jnp.where(jnp.asarray(live)[None, :], jnp.cos(ang), 1.0)
    sin = jnp.asarray(sign)[None, :] * jnp.sin(ang)
    ctab = jnp.concatenate([cos, jnp.ones((tile, LANES), F32)], axis=0)
    stab = jnp.concatenate([sin, jnp.zeros((tile, LANES), F32)], axis=0)
    return ctab, stab


def _gqa_layout():
    q = C_HEAD_DIM // 4
    perm = np.concatenate([np.arange(0, q), np.arange(2 * q, 3 * q), np.arange(q, 2 * q),
                           np.arange(3 * q, 4 * q)])
    slots = [(axis, j, sign) for sign in (-1.0, 1.0) for axis in (0, 1) for j in range(q)]
    return perm, slots


def _diff_layout():
    e = D_QK_DIM // 4
    perm, slots = [], []
    for half, sign in ((0, -1.0), (1, 1.0)):
        for comp in range(2):
            for axis in range(2):
                perm.extend(comp * D_QK_DIM + axis * 2 * e + half * e + j for j in range(e))
                slots.extend((axis, j, sign) for j in range(e))
    return np.asarray(perm), slots


def _mla_rope_layout():
    e = B_ROPE // 4
    dims, slots = [], []
    for half, sign in ((0, -1.0), (1, 1.0)):
        for hd in range(2):
            for axis in range(2):
                dims.extend((hd, axis * 2 * e + half * e + j) for j in range(e))
                slots.extend((axis, j, sign) for j in range(e))
        dims.extend([None] * (2 * B_ROPE // 2))
        slots.extend([None] * (2 * B_ROPE // 2))
    return dims, slots


def _gather_cols(w, cols):
    idx = np.asarray([0 if c is None else c for c in cols])
    keep = np.asarray([c is not None for c in cols])
    out = jnp.take(w, jnp.asarray(idx), axis=1)
    return jnp.where(jnp.asarray(keep)[None, :], out, 0).astype(BF16)


def kernel(x, c, ctx, c_ctx, mod_w, mod_b, norm_g, a_w_in, a_lb, a_onorm_g, a_w_out, b_w_in, b_qnorm_g, b_kvnorm_g, b_w_qb, b_w_kvb, b_w_out, c_w_in, c_qnorm_g, c_knorm_g, c_w_out, d_w_in, d_lambda, d_onorm_g, d_w_out, ffn_w_gu, ffn_w_down, moe_router, moe_w_gu, moe_w_down):
    batch, seq, d = x.shape
    ctx_len = ctx.shape[1]
    depth = mod_w.shape[0]
    n_lat = batch * seq
    lay_all = Layout(batch, seq, ctx_len, ROW_TILE, with_ctx=True)
    lay_lat = Layout(batch, seq, ctx_len, ROW_TILE, with_ctx=False)
    lay_proj = Layout(batch, seq, ctx_len, PROJ_TILE, with_ctx=True)

    h = jnp.concatenate([x.reshape(n_lat, d), ctx.reshape(batch * ctx_len, d)], axis=0)

    n_rows = 8 * ((batch + 1 + 7) // 8)
    cond = jnp.zeros((n_rows, d), F32).at[0].set(c_ctx).at[1:batch + 1].set(c)
    mod_all = modvec(cond, mod_w, mod_b)

    lower_bounds = jnp.cumsum(jax.nn.softmax(a_lb.astype(F32), axis=1), axis=1)

    for i in range(depth):
        last = i == depth - 1
        kind, j = i % N_MIXERS, i // N_MIXERS
        modt = mod_all[i].reshape(n_rows * N_MOD, 1, d)
        lay_out = lay_lat if last else lay_all
        g0, g1, g2, g3 = norm_g[i, 0], norm_g[i, 1], norm_g[i, 2], norm_g[i, 3]

        if kind == 0:
            z = project(h, g0, a_w_in[j].astype(BF16), lay_proj, modt=modt, which=(0, 1),
                        tn=A_HEADS * LANES, name="gla_in")
            s0 = jnp.zeros((batch, A_HEADS, 2, LANES, LANES), F32)
            o, s_ctx = gla_scan(z, lower_bounds[0, i], lower_bounds[1, i], s0, None,
                                batch=batch, seg_len=ctx_len, row0=n_lat)
            o, _ = gla_scan(z, lower_bounds[0, i], lower_bounds[1, i], s_ctx, o,
                            batch=batch, seg_len=seq, row0=0)
            h = out_project(o, a_w_out[j].astype(BF16), h, g1, modt, lay_out,
                            gla_gate=(z, a_onorm_g[j], 4), name="gla_out")
        else:
            if kind == 1:
                dims, slots = _mla_rope_layout()
                rope = _rope_tables(seq, lay_proj.tile, B_ROPE // 4, slots)
                w_in = b_w_in[j]
                kr_cols = [None if dd is None else B_Q_RANK + B_KV_RANK + dd[1] for dd in dims]
                w_cat = jnp.concatenate(
                    [w_in[:, :B_Q_RANK].astype(BF16), _gather_cols(w_in, kr_cols),
                     w_in[:, B_Q_RANK:B_Q_RANK + B_KV_RANK].astype(BF16)], axis=1)
                nq_blocks = B_Q_RANK // LANES
                n_in_blocks = nq_blocks + 1 + B_KV_RANK // LANES
                blocks = [Blk(cb, rope=(cb == nq_blocks)) for cb in range(n_in_blocks)]
                cin = project(h, g0, w_cat, lay_proj, modt=modt, which=(0, 1), blocks=blocks,
                              rope=rope, name="mla_in")
                hw = B_NOPE + B_ROPE
                n_pairs = B_HEADS // 2
                qn_cols = [hd * hw + t for hd in range(B_HEADS) for t in range(B_NOPE)]
                qr_cols = [None if dd is None else (2 * p + dd[0]) * hw + B_NOPE + dd[1]
                           for p in range(n_pairs) for dd in dims]
                w_q = _gather_cols(b_w_qb[j], qn_cols + qr_cols)
                qmul = (B_NOPE + B_ROPE) ** -0.5 * LOG2E
                q_all = project(cin, b_qnorm_g[j], w_q, lay_proj, x_block=0,
                                blocks=([Blk(p, mult=qmul) for p in range(n_pairs)]
                                        + [Blk(n_pairs + p, rope=True, mult=qmul)
                                           for p in range(n_pairs)]),
                                rope=rope, out_dtype=BF16, name="mla_q")
                kw = B_NOPE + B_V
                kn_cols = [hd * kw + t for hd in range(B_HEADS) for t in range(B_NOPE)]
                v_cols = [hd * kw + B_NOPE + t for hd in range(B_HEADS) for t in range(B_V)]
                w_kv = _gather_cols(b_w_kvb[j], kn_cols + v_cols)
                kblocks = [blk for p in range(n_pairs) for blk in (Blk(p), Blk("copy"))]
                kblocks += [Blk(n_pairs + p) for p in range(n_pairs)]
                kv_all = project(cin, b_kvnorm_g[j], w_kv, lay_proj,
                                 x_block=(B_Q_RANK + LANES) // B_KV_RANK, blocks=kblocks,
                                 copy_src=(cin, nq_blocks), out_dtype=BF16, name="mla_kv")
                srcs, akind, extra, out_scale = (q_all, kv_all), "mla", (), 1.0
                w_out = b_w_out[j]
            elif kind == 2:
                perm, slots = _gqa_layout()
                rope = _rope_tables(seq, lay_proj.tile, C_HEAD_DIM // 4, slots)
                nqk = C_HEADS + C_KV_HEADS
                cols = [hd * C_HEAD_DIM + p for hd in range(nqk) for p in perm]
                cols += list(range(nqk * C_HEAD_DIM, (nqk + C_KV_HEADS) * C_HEAD_DIM))
                w_cat = _gather_cols(c_w_in[j], cols)
                gains = jnp.zeros((8, LANES), F32).at[0].set(c_qnorm_g[j][perm]).at[1].set(
                    c_knorm_g[j][perm])
                qmul = C_HEAD_DIM ** -0.5 * LOG2E
                blocks = ([Blk(hd, gain=0, rope=True, mult=qmul) for hd in range(C_HEADS)]
                          + [Blk(C_HEADS + hd, gain=1, rope=True) for hd in range(C_KV_HEADS)]
                          + [Blk(nqk + hd) for hd in range(C_KV_HEADS)])
                qkv = project(h, g0, w_cat, lay_proj, modt=modt, which=(0, 1), blocks=blocks,
                              head_gain=gains, rope=rope, out_dtype=BF16, name="gqa_in")
                srcs, akind, extra, out_scale = (qkv,), "gqa", (), 1.0
                w_out = c_w_out[j]
            else:
                perm, slots = _diff_layout()
                rope = _rope_tables(seq, lay_proj.tile, D_QK_DIM // 4, slots)
                nqk = 2 * D_HEADS
                cols = [hd * 2 * D_QK_DIM + p for hd in range(nqk) for p in perm]
                cols += list(range(nqk * 2 * D_QK_DIM, (nqk + D_HEADS) * 2 * D_QK_DIM))
                w_cat = _gather_cols(d_w_in[j], cols)
                qmul = D_QK_DIM ** -0.5 * LOG2E
                blocks = ([Blk(hd, rope=True, mult=qmul) for hd in range(D_HEADS)]
                          + [Blk(D_HEADS + hd, rope=True) for hd in range(D_HEADS)]
                          + [Blk(nqk + hd) for hd in range(D_HEADS)])
                qkv = project(h, g0, w_cat, lay_proj, modt=modt, which=(0, 1), blocks=blocks,
                              rope=rope, out_dtype=BF16, name="diff_in")
                lambda_init = 0.8 - 0.6 * math.exp(-0.3 * i)
                lam = d_lambda[j].astype(F32)
                lam_full = (jnp.exp(jnp.sum(lam[0] * lam[1])) - jnp.exp(jnp.sum(lam[2] * lam[3]))
                            + lambda_init)
                extra = (jnp.full((1, LANES), lam_full, F32), d_onorm_g[j])
                srcs, akind, out_scale = (qkv,), "diff", 1.0 - lambda_init
                w_out = d_w_out[j]
            a = attention(akind, srcs, None, batch=batch, seq=seq, ctx_len=ctx_len,
                          queries="lat", extra=extra, out_scale=out_scale)
            if not last:
                a = attention(akind, srcs, a, batch=batch, seq=seq, ctx_len=ctx_len,
                              queries="ctx", extra=extra, out_scale=out_scale)
            h = out_project(a, w_out.astype(BF16), h, g1, modt, lay_out, name=f"{akind}_out")

        if i % 2 == 0:
            h = ffn_residual(h, g2, g3, ffn_w_gu[i // 2].astype(BF16),
                             ffn_w_down[i // 2].astype(BF16), modt, lay_out)
        else:
            h = moe_residual(h, g2, g3, moe_router[i // 2], moe_w_gu[i // 2].astype(BF16),
                             moe_w_down[i // 2].astype(BF16), modt, lay_out)
    return h[:n_lat].reshape(batch, seq, d)
```

```python
import functools
import math
from typing import Any, NamedTuple

import jax
import jax.numpy as jnp
import numpy as np
from jax import lax
from jax.experimental import pallas as pl
from jax.experimental.pallas import tpu as pltpu

F32 = jnp.float32
BF16 = jnp.bfloat16

N_MIXERS = 4
N_MOD = 6
GRID_W = 64
ROPE_THETA = 10000.0
NORM_EPS = 1e-6
A_HEADS = 8
A_DK = 128
A_CHUNK = 32
B_HEADS = 16
B_Q_RANK = 384
B_KV_RANK = 256
B_NOPE = 64
B_ROPE = 32
B_V = 64
C_HEADS = 8
C_KV_HEADS = 2
C_HEAD_DIM = 128
D_HEADS = 8
D_QK_DIM = 64
N_EXPERTS = 8
TOP_K = 2

LANES = 128
V7X_VMEM_BYTES = 64 * 1024 * 1024
VMEM_LIMIT = V7X_VMEM_BYTES - 8 * 1024 * 1024

ROW_TILE = 1024
PROJ_TILE = 512
Q_TILE = {"gqa": 256, "mla": 512, "diff": 512}
KEY_BLOCK = 512
LOG2E = 1.4426950408889634
GLA_TILE = 256
FF_CHUNK = 512
MOE_TILE = 1024


def _cparams(*sem):
    return pltpu.CompilerParams(dimension_semantics=sem, vmem_limit_bytes=VMEM_LIMIT)


def _rms(x, g):
    return x * lax.rsqrt(jnp.mean(x * x, axis=-1, keepdims=True) + NORM_EPS) * g


def _silu(x):
    return x * jax.nn.sigmoid(x)


class Layout:
    def __init__(self, batch, seq, ctx_len, tile_cap, with_ctx=True):
        tile = math.gcd(math.gcd(seq, batch * ctx_len), tile_cap)
        self.batch, self.seq, self.ctx_len, self.tile = batch, seq, ctx_len, tile
        self.n_lat = batch * seq
        self.n_ctx = batch * ctx_len if with_ctx else 0
        self.rows = self.n_lat + self.n_ctx
        assert seq % tile == 0 and self.n_ctx % tile == 0
        self.lat_tiles = self.n_lat // tile
        self.tiles = self.rows // tile
        self.tiles_per_batch = seq // tile

    def mod_row(self, i):
        return jnp.where(i < self.lat_tiles, 1 + i // self.tiles_per_batch, 0)

    def rope_block(self, i):
        return jnp.where(i < self.lat_tiles, i % self.tiles_per_batch, self.tiles_per_batch)

    def mod_spec(self, which, d):
        return pl.BlockSpec((1, 1, d), lambda i, *_: (self.mod_row(i) * N_MOD + which, 0, 0))


def _modvec_kernel(c_ref, w_ref, b_ref, o_ref):
    cond = _silu(c_ref[...]).astype(BF16)
    o_ref[0] = jnp.dot(cond, w_ref[0].astype(BF16), preferred_element_type=F32) + b_ref[0]


def modvec(cond_rows, mod_w, mod_b):
    depth, d, n = mod_w.shape
    r = cond_rows.shape[0]
    tn = d
    return pl.pallas_call(
        _modvec_kernel,
        out_shape=jax.ShapeDtypeStruct((depth, r, n), F32),
        grid=(depth, n // tn),
        in_specs=[pl.BlockSpec((r, d), lambda l, j: (0, 0)),
                  pl.BlockSpec((1, d, tn), lambda l, j: (l, 0, j)),
                  pl.BlockSpec((1, 1, tn), lambda l, j: (l, 0, j))],
        out_specs=pl.BlockSpec((1, r, tn), lambda l, j: (l, 0, j)),
        compiler_params=_cparams("parallel", "parallel"),
        name="modvec",
    )(cond_rows, mod_w, mod_b.reshape(depth, 1, n))


class Blk(NamedTuple):
    src: Any
    gain: Any = None
    rope: bool = False
    mult: Any = None


def _proj_kernel(*refs, has_mod, blocks, vt_srcs, has_gain, has_rope, has_copy):
    it = iter(refs)
    x_ref, g_ref = next(it), next(it)
    sh_ref = sc_ref = hg_ref = c_ref = s_ref = cp_ref = vt_ref = None
    if has_mod:
        sh_ref, sc_ref = next(it), next(it)
    w_ref = next(it)
    if has_gain:
        hg_ref = next(it)
    if has_rope:
        c_ref, s_ref = next(it), next(it)
    if has_copy:
        cp_ref = next(it)
    o_ref = next(it)
    if vt_srcs:
        vt_ref = next(it)
    u_ref = next(it)

    @pl.when(pl.program_id(1) == 0)
    def _():
        y = _rms(x_ref[...].astype(F32), g_ref[...])
        if has_mod:
            y = y * (1 + sc_ref[0]) + sh_ref[0]
        u_ref[...] = y.astype(BF16)

    acc = jnp.dot(u_ref[...], w_ref[...], preferred_element_type=F32)
    if blocks is None:
        o_ref[...] = acc.astype(o_ref.dtype)
        return
    for c, blk in enumerate(blocks):
        if blk.src == "copy":
            a = cp_ref[...].astype(F32)
        else:
            a = acc[:, blk.src * LANES:(blk.src + 1) * LANES]
        if blk.gain is not None:
            a = _rms(a, hg_ref[blk.gain:blk.gain + 1, :])
        if blk.rope:
            a = a * c_ref[...] + pltpu.roll(a, LANES // 2, 1) * s_ref[...]
        if blk.mult is not None:
            a = a * blk.mult
        o_ref[:, c * LANES:(c + 1) * LANES] = a.astype(o_ref.dtype)
    for c, src in enumerate(vt_srcs):
        a = acc[:, src * LANES:(src + 1) * LANES]
        vt_ref[c * LANES:(c + 1) * LANES, :] = a.T.astype(vt_ref.dtype)


def project(x, g, w, lay, *, modt=None, which=None, x_block=0, tn=None, blocks=None, vt_srcs=(),
            head_gain=None, rope=None, copy_src=None, out_dtype=F32, name="project"):
    k, n = w.shape
    tm = lay.tile
    tn = n if tn is None else tn
    assert blocks is None or tn == n
    n_out = n if blocks is None else len(blocks) * LANES
    in_specs = [pl.BlockSpec((tm, k), lambda i, j: (i, x_block)),
                pl.BlockSpec((1, k), lambda i, j: (0, 0))]
    args = [x, g.reshape(1, k)]
    if modt is not None:
        in_specs += [lay.mod_spec(which[0], k), lay.mod_spec(which[1], k)]
        args += [modt, modt]
    in_specs.append(pl.BlockSpec((k, tn), lambda i, j: (0, j)))
    args.append(w)
    if head_gain is not None:
        in_specs.append(pl.BlockSpec(head_gain.shape, lambda i, j: (0, 0)))
        args.append(head_gain)
    if rope is not None:
        spec = pl.BlockSpec((tm, LANES), lambda i, j: (lay.rope_block(i), 0))
        in_specs += [spec, spec]
        args += list(rope)
    if copy_src is not None:
        cp_arr, cp_block = copy_src
        in_specs.append(pl.BlockSpec((tm, LANES), lambda i, j: (i, cp_block)))
        args.append(cp_arr)
    out_shape = [jax.ShapeDtypeStruct((lay.rows, n_out), out_dtype)]
    out_specs = [pl.BlockSpec((tm, n_out if blocks is not None else tn), lambda i, j: (i, j))]
    if vt_srcs:
        out_shape.append(jax.ShapeDtypeStruct((len(vt_srcs) * LANES, lay.rows), BF16))
        out_specs.append(pl.BlockSpec((len(vt_srcs) * LANES, tm), lambda i, j: (0, i)))
    kern = functools.partial(_proj_kernel, has_mod=modt is not None, blocks=blocks,
                             vt_srcs=tuple(vt_srcs), has_gain=head_gain is not None,
                             has_rope=rope is not None, has_copy=copy_src is not None)
    outs = pl.pallas_call(
        kern,
        out_shape=out_shape,
        grid=(lay.tiles, n // tn),
        in_specs=in_specs,
        out_specs=out_specs,
        scratch_shapes=[pltpu.VMEM((tm, k), BF16)],
        compiler_params=_cparams("parallel", "arbitrary"),
        name=name,
    )(*args)
    return outs if vt_srcs else outs[0]


def _oproj_kernel(a_ref, w_ref, h_ref, g_ref, gate_ref, o_ref):
    y = jnp.dot(a_ref[...], w_ref[...], preferred_element_type=F32)
    o_ref[...] = h_ref[...] + gate_ref[0] * _rms(y, g_ref[...])


def _oproj_gla_kernel(o_in_ref, gz_ref, og_ref, w_ref, h_ref, g_ref, gate_ref, o_ref):
    o = o_in_ref[...]
    gz = gz_ref[...]
    parts = []
    for hd in range(A_HEADS):
        sl = slice(hd * LANES, (hd + 1) * LANES)
        parts.append((_rms(o[:, sl], og_ref[...]) * _silu(gz[:, sl])).astype(BF16))
    a = jnp.concatenate(parts, axis=1)
    y = jnp.dot(a, w_ref[...], preferred_element_type=F32)
    o_ref[...] = h_ref[...] + gate_ref[0] * _rms(y, g_ref[...])


def out_project(a, w, h, g, modt, lay, *, gla_gate=None, name="out_project"):
    k, d = w.shape
    tm = lay.tile
    row = lambda i: (i, 0)
    if gla_gate is None:
        kern = _oproj_kernel
        in_specs = [pl.BlockSpec((tm, k), row)]
        args = [a]
    else:
        z, onorm_g, gate_block = gla_gate
        kern = _oproj_gla_kernel
        in_specs = [pl.BlockSpec((tm, k), row),
                    pl.BlockSpec((tm, k), lambda i: (i, gate_block)),
                    pl.BlockSpec((1, LANES), lambda i: (0, 0))]
        args = [a, z, onorm_g.reshape(1, LANES)]
    in_specs += [pl.BlockSpec((k, d), lambda i: (0, 0)),
                 pl.BlockSpec((tm, d), row),
                 pl.BlockSpec((1, d), lambda i: (0, 0)),
                 lay.mod_spec(2, d)]
    args += [w, h, g.reshape(1, d), modt]
    return pl.pallas_call(
        kern,
        out_shape=jax.ShapeDtypeStruct((lay.rows, d), F32),
        grid=(lay.tiles,),
        in_specs=in_specs,
        out_specs=pl.BlockSpec((tm, d), row),
        compiler_params=_cparams("parallel"),
        name=name,
    )(*args)


def _ffn_kernel(h_ref, g2_ref, sh_ref, sc_ref, wg_ref, wu_ref, wd_ref, g3_ref, gate_ref, o_ref,
                u_ref, acc_ref):
    j = pl.program_id(1)

    @pl.when(j == 0)
    def _():
        y = _rms(h_ref[...], g2_ref[...]) * (1 + sc_ref[0]) + sh_ref[0]
        u_ref[...] = y.astype(BF16)
        acc_ref[...] = jnp.zeros_like(acc_ref)

    u = u_ref[...]
    a = jnp.dot(u, wg_ref[...], preferred_element_type=F32)
    b = jnp.dot(u, wu_ref[...], preferred_element_type=F32)
    hid = (_silu(a) * b).astype(BF16)
    acc_ref[...] += jnp.dot(hid, wd_ref[...], preferred_element_type=F32)

    @pl.when(j == pl.num_programs(1) - 1)
    def _():
        o_ref[...] = h_ref[...] + gate_ref[0] * _rms(acc_ref[...], g3_ref[...])


def ffn_residual(h, g2, g3, w_gu, w_down, modt, lay):
    d = h.shape[1]
    ff = w_down.shape[0]
    fc = FF_CHUNK
    nfc = ff // fc
    tm = lay.tile
    row = lambda i, j: (i, 0)
    vec = pl.BlockSpec((1, d), lambda i, j: (0, 0))
    return pl.pallas_call(
        _ffn_kernel,
        out_shape=jax.ShapeDtypeStruct((lay.rows, d), F32),
        grid=(lay.tiles, nfc),
        in_specs=[pl.BlockSpec((tm, d), row), vec, lay.mod_spec(3, d), lay.mod_spec(4, d),
                  pl.BlockSpec((d, fc), lambda i, j: (0, j)),
                  pl.BlockSpec((d, fc), lambda i, j: (0, j + nfc)),
                  pl.BlockSpec((fc, d), lambda i, j: (j, 0)),
                  vec, lay.mod_spec(5, d)],
        out_specs=pl.BlockSpec((tm, d), row),
        scratch_shapes=[pltpu.VMEM((tm, d), BF16), pltpu.VMEM((tm, d), F32)],
        compiler_params=_cparams("parallel", "arbitrary"),
        name="ffn_residual",
    )(h, g2.reshape(1, d), modt, modt, w_gu, w_gu, w_down, g3.reshape(1, d), modt)


def _router_kernel(h_ref, g2_ref, sh_ref, sc_ref, r_ref, u_ref, ids_ref, wts_ref):
    u = (_rms(h_ref[...], g2_ref[...]) * (1 + sc_ref[0]) + sh_ref[0]).astype(BF16)
    u_ref[...] = u
    logits = jnp.dot(u, r_ref[...], preferred_element_type=F32)
    lane = lax.broadcasted_iota(jnp.int32, logits.shape, 1)
    neg = jnp.float32(-jnp.inf)
    logits = jnp.where(lane < N_EXPERTS, logits, neg)
    m1 = jnp.max(logits, axis=-1, keepdims=True)
    i1 = jnp.min(jnp.where(logits == m1, lane, LANES), axis=-1, keepdims=True)
    rest = jnp.where(lane == i1, neg, logits)
    m2 = jnp.max(rest, axis=-1, keepdims=True)
    i2 = jnp.min(jnp.where(rest == m2, lane, LANES), axis=-1, keepdims=True)
    e = jnp.exp(m2 - m1)
    w1 = 1.0 / (1.0 + e)
    w2 = e / (1.0 + e)
    ids_ref[...] = jnp.where(lane == 0, i1, jnp.where(lane == 1, i2, 0))
    wts_ref[...] = jnp.where(lane == 0, w1, jnp.where(lane == 1, w2, 0.0))


def moe_router(h, g2, router, modt, lay):
    d = h.shape[1]
    tm = lay.tile
    r_pad = jnp.zeros((d, LANES), BF16).at[:, :N_EXPERTS].set(router.astype(BF16))
    row = lambda i: (i, 0)
    return pl.pallas_call(
        _router_kernel,
        out_shape=(jax.ShapeDtypeStruct((lay.rows, d), BF16),
                   jax.ShapeDtypeStruct((lay.rows, LANES), jnp.int32),
                   jax.ShapeDtypeStruct((lay.rows, LANES), F32)),
        grid=(lay.tiles,),
        in_specs=[pl.BlockSpec((tm, d), row), pl.BlockSpec((1, d), lambda i: (0, 0)),
                  lay.mod_spec(3, d), lay.mod_spec(4, d),
                  pl.BlockSpec((d, LANES), lambda i: (0, 0))],
        out_specs=(pl.BlockSpec((tm, d), row), pl.BlockSpec((tm, LANES), row),
                   pl.BlockSpec((tm, LANES), row)),
        compiler_params=_cparams("parallel"),
        name="moe_router",
    )(h, g2.reshape(1, d), modt, modt, r_pad)


def _moe_ffn_kernel(te_ref, nu_ref, x_ref, wg_ref, wu_ref, wd_ref, o_ref, acc_ref):
    t, j = pl.program_id(0), pl.program_id(1)

    @pl.when(j == 0)
    def _():
        acc_ref[...] = jnp.zeros_like(acc_ref)

    @pl.when(t < nu_ref[0])
    def _():
        x = x_ref[...]
        a = jnp.dot(x, wg_ref[0].astype(BF16), preferred_element_type=F32)
        b = jnp.dot(x, wu_ref[0].astype(BF16), preferred_element_type=F32)
        hid = (_silu(a) * b).astype(BF16)
        acc_ref[...] += jnp.dot(hid, wd_ref[0].astype(BF16), preferred_element_type=F32)

    @pl.when(j == pl.num_programs(1) - 1)
    def _():
        o_ref[...] = acc_ref[...]


def moe_ffn(x_sorted, tile_expert, n_used, w_gu, w_down):
    n_slots, d = x_sorted.shape
    ff = w_down.shape[1]
    fc = FF_CHUNK
    nfc = ff // fc
    tm = MOE_TILE
    n_tiles = n_slots // tm

    def jj(t, j, te, nu):
        return jnp.where(t < nu[0], j, nfc - 1)

    grid_spec = pltpu.PrefetchScalarGridSpec(
        num_scalar_prefetch=2,
        grid=(n_tiles, nfc),
        in_specs=[pl.BlockSpec((tm, d), lambda t, j, te, nu: (t, 0)),
                  pl.BlockSpec((1, d, fc), lambda t, j, te, nu: (te[t], 0, jj(t, j, te, nu))),
                  pl.BlockSpec((1, d, fc), lambda t, j, te, nu: (te[t], 0, jj(t, j, te, nu) + nfc)),
                  pl.BlockSpec((1, fc, d), lambda t, j, te, nu: (te[t], jj(t, j, te, nu), 0))],
        out_specs=pl.BlockSpec((tm, d), lambda t, j, te, nu: (t, 0)),
        scratch_shapes=[pltpu.VMEM((tm, d), F32)],
    )
    return pl.pallas_call(
        _moe_ffn_kernel,
        out_shape=jax.ShapeDtypeStruct((n_slots, d), F32),
        grid_spec=grid_spec,
        compiler_params=_cparams("arbitrary", "arbitrary"),
        name="moe_ffn",
    )(tile_expert, n_used, x_sorted, w_gu, w_gu, w_down)


def _combine_kernel(y_ref, w_ref, h_ref, g_ref, gate_ref, o_ref):
    w = w_ref[...]
    y = w[:, 0:1] * y_ref[0] + w[:, 1:2] * y_ref[1]
    o_ref[...] = h_ref[...] + gate_ref[0] * _rms(y, g_ref[...])


def combine_residual(y_pair, wts, h, g3, modt, lay):
    d = h.shape[1]
    tm = lay.tile
    row = lambda i: (i, 0)
    return pl.pallas_call(
        _combine_kernel,
        out_shape=jax.ShapeDtypeStruct((lay.rows, d), F32),
        grid=(lay.tiles,),
        in_specs=[pl.BlockSpec((2, tm, d), lambda i: (0, i, 0)), pl.BlockSpec((tm, LANES), row),
                  pl.BlockSpec((tm, d), row), pl.BlockSpec((1, d), lambda i: (0, 0)),
                  lay.mod_spec(5, d)],
        out_specs=pl.BlockSpec((tm, d), row),
        compiler_params=_cparams("parallel"),
        name="moe_combine",
    )(y_pair, wts, h, g3.reshape(1, d), modt)


def moe_residual(h, g2, g3, router, w_gu, w_down, modt, lay):
    rows = lay.rows
    u, ids, wts = moe_router(h, g2, router, modt, lay)
    e_flat = ids[:, :TOP_K].reshape(-1)
    n = rows * TOP_K
    onehot = (e_flat[:, None] == jnp.arange(N_EXPERTS, dtype=jnp.int32)[None, :]).astype(jnp.int32)
    csum = jnp.cumsum(onehot, axis=0)
    counts = csum[-1]
    rank = jnp.sum(csum * onehot, axis=1) - 1
    padded = ((counts + MOE_TILE - 1) // MOE_TILE) * MOE_TILE
    ends = jnp.cumsum(padded)
    starts = ends - padded
    dest = jnp.sum(starts[None, :] * onehot, axis=1) + rank
    n_slots = ((n + N_EXPERTS * (MOE_TILE - 1)) // MOE_TILE + 1) * MOE_TILE
    n_tiles = n_slots // MOE_TILE
    src_tok = jnp.zeros((n_slots,), jnp.int32).at[dest].set(
        jnp.arange(n, dtype=jnp.int32) // TOP_K, unique_indices=True)
    n_used = (ends[-1] // MOE_TILE).astype(jnp.int32).reshape(1)
    tile_start = jnp.arange(n_tiles, dtype=jnp.int32) * MOE_TILE
    tile_expert = jnp.minimum(
        jnp.sum((ends[None, :] <= tile_start[:, None]).astype(jnp.int32), axis=1), N_EXPERTS - 1)
    last_e = jnp.max(jnp.where(counts > 0, jnp.arange(N_EXPERTS, dtype=jnp.int32), 0))
    tile_expert = jnp.where(tile_start < ends[-1], tile_expert, last_e)
    x_sorted = jnp.take(u, src_tok, axis=0, mode="clip")
    y_sorted = moe_ffn(x_sorted, tile_expert, n_used, w_gu, w_down)
    y_pair = jnp.take(y_sorted, dest.reshape(rows, TOP_K).T, axis=0, mode="clip")
    return combine_residual(y_pair, wts, h, g3, modt, lay)


def _gla_kernel(*refs, n_tiles, has_alias):
    q_ref, ff_ref, fb_ref, v_ref, lbf_ref, lbb_ref, s0_ref = refs[:7]
    o_ref, st_ref, s_scr, ob_scr = refs[8:] if has_alias else refs[7:]
    rt = GLA_TILE
    nch = rt // A_CHUNK
    s_scr[...] = s0_ref[0, 0]
    row = lax.broadcasted_iota(jnp.int32, (rt, LANES), 0) % A_CHUNK
    ci = lax.broadcasted_iota(jnp.int32, (rt, rt), 0)
    si = lax.broadcasted_iota(jnp.int32, (rt, rt), 1)
    same_chunk = (ci // A_CHUNK) == (si // A_CHUNK)
    nt = (((1,), (1,)), ((), ()))

    def direction(r0, f_ref, lb_ref, d, reverse):
        rows = pl.ds(pl.multiple_of(r0, rt), rt)
        lb = lb_ref[...]
        f = lb + (1.0 - lb) * jax.nn.sigmoid(f_ref[rows, :])
        logf = jnp.log(f)
        kk = 1.0 - f
        q = q_ref[rows, :] * (A_DK ** -0.5)
        v = v_ref[rows, :]
        cum = logf
        for s in (1, 2, 4, 8, 16):
            if reverse:
                cum = cum + jnp.where(row < A_CHUNK - s, pltpu.roll(cum, rt - s, 0), 0.0)
            else:
                cum = cum + jnp.where(row >= s, pltpu.roll(cum, s, 0), 0.0)
        cum3 = cum.reshape(nch, A_CHUNK, LANES)
        end3 = cum3[:, 0:1, :] if reverse else cum3[:, A_CHUNK - 1:A_CHUNK, :]
        q_dec = (q * jnp.exp(cum)).astype(BF16)
        k_neg = (kk * jnp.exp(-cum)).astype(BF16)
        k_end = (kk.reshape(nch, A_CHUNK, LANES) * jnp.exp(end3 - cum3)).astype(BF16)
        vb = v.astype(BF16)
        decay = jnp.exp(end3)
        att = lax.dot_general(q_dec, k_neg, nt, preferred_element_type=F32)
        keep = same_chunk & ((si >= ci) if reverse else (si <= ci))
        o_intra = jnp.dot(jnp.where(keep, att, 0.0).astype(BF16), vb, preferred_element_type=F32)
        q_dec3 = q_dec.reshape(nch, A_CHUNK, LANES)
        v3 = vb.reshape(nch, A_CHUNK, LANES)
        upds = [lax.dot_general(v3[g], k_end[g], (((0,), (0,)), ((), ())),
                                preferred_element_type=F32) for g in range(nch)]
        before = [None] * nch
        state = s_scr[d]
        for g in (range(nch - 1, -1, -1) if reverse else range(nch)):
            before[g] = state.astype(BF16)
            state = decay[g] * state + upds[g]
        s_scr[d] = state
        outs = [lax.dot_general(q_dec3[g], before[g], nt, preferred_element_type=F32)
                for g in range(nch)]
        return o_intra + jnp.concatenate(outs, axis=0), rows

    def body(t, carry):
        o_f, rows_f = direction(t * rt, ff_ref, lbf_ref, 0, False)
        o_ref[rows_f, :] = o_f
        o_b, rows_b = direction((n_tiles - 1 - t) * rt, fb_ref, lbb_ref, 1, True)
        ob_scr[rows_b, :] = o_b
        return carry

    lax.fori_loop(0, n_tiles, body, 0)
    o_ref[...] += ob_scr[...]
    st_ref[0, 0] = s_scr[...]


def gla_scan(z, lb_f, lb_b, s0, o_alias, *, batch, seg_len, row0):
    rows = z.shape[0]
    n_tiles = seg_len // GLA_TILE
    rb0 = row0 // seg_len
    assert row0 % seg_len == 0

    def col(kind):
        return pl.BlockSpec((seg_len, LANES), lambda b, h: (rb0 + b, kind * A_HEADS + h))

    lbspec = pl.BlockSpec((1, LANES), lambda b, h: (0, h))
    sspec = pl.BlockSpec((1, 1, 2, LANES, LANES), lambda b, h: (b, h, 0, 0, 0))
    has_alias = o_alias is not None
    kern = functools.partial(_gla_kernel, n_tiles=n_tiles, has_alias=has_alias)
    in_specs = [col(0), col(1), col(2), col(3), lbspec, lbspec, sspec]
    args = [z, z, z, z, lb_f.reshape(1, -1), lb_b.reshape(1, -1), s0]
    if has_alias:
        in_specs.append(pl.BlockSpec(memory_space=pl.ANY))
        args.append(o_alias)
    return pl.pallas_call(
        kern,
        out_shape=(jax.ShapeDtypeStruct((rows, A_HEADS * LANES), F32),
                   jax.ShapeDtypeStruct(s0.shape, F32)),
        grid=(batch, A_HEADS),
        in_specs=in_specs,
        out_specs=(pl.BlockSpec((seg_len, LANES), lambda b, h: (rb0 + b, h)), sspec),
        scratch_shapes=[pltpu.VMEM((2, LANES, LANES), F32), pltpu.VMEM((seg_len, LANES), F32)],
        input_output_aliases={7: 0} if has_alias else {},
        compiler_params=_cparams("parallel", "parallel"),
        name="gla_scan",
    )(*args)


def _flash(q, kc, vc, kl_ref, vl_ref):
    nt = (((1,), (1,)), ((), ()))
    n_kblocks = 0 if kl_ref is None else kl_ref.shape[0] // KEY_BLOCK
    m = l = acc = None
    for blk in range(n_kblocks + 1):
        if blk == 0:
            kb, vb = kc, vc
        else:
            ks = slice((blk - 1) * KEY_BLOCK, blk * KEY_BLOCK)
            kb, vb = kl_ref[ks, :], vl_ref[ks, :]
        s = lax.dot_general(q, kb, nt, preferred_element_type=F32)
        m_blk = jnp.max(s, axis=-1, keepdims=True)
        m_new = m_blk if blk == 0 else jnp.maximum(m, m_blk)
        p = jnp.exp2(s - m_new)
        psum = p[:, :LANES]
        for t in range(1, p.shape[1] // LANES):
            psum = psum + p[:, t * LANES:(t + 1) * LANES]
        pv = jnp.dot(p.astype(BF16), vb, preferred_element_type=F32)
        if blk == 0:
            l, acc = psum, pv
        else:
            alpha = jnp.exp2(m - m_new)
            l = alpha * l + psum
            acc = alpha * acc + pv
        m = m_new
    return acc / jnp.sum(l, axis=-1, keepdims=True)


def _split_attn_refs(refs, n_lead, has_lat):
    lead = refs[:n_lead]
    kc_ref, vc_ref = refs[n_lead], refs[n_lead + 1]
    if has_lat:
        kl_ref, vl_ref, o_ref = refs[n_lead + 2:]
    else:
        kl_ref = vl_ref = None
        _, o_ref = refs[n_lead + 2:]
    return lead, kc_ref, vc_ref, kl_ref, vl_ref, o_ref


def _attn_gqa_kernel(*refs, has_lat):
    (q_ref,), kc_ref, vc_ref, kl_ref, vl_ref, o_ref = _split_attn_refs(refs, 1, has_lat)
    group = C_HEADS // C_KV_HEADS
    tq = q_ref.shape[0]
    q = jnp.concatenate([q_ref[:, g * LANES:(g + 1) * LANES] for g in range(group)], axis=0)
    o = _flash(q, kc_ref[...], vc_ref[...], kl_ref, vl_ref)
    for g in range(group):
        o_ref[:, g * LANES:(g + 1) * LANES] = o[g * tq:(g + 1) * tq].astype(o_ref.dtype)


def _attn_mla_kernel(*refs, has_lat):
    (qn_ref, qr_ref), kc_ref, vc_ref, kl_ref, vl_ref, o_ref = _split_attn_refs(refs, 2, has_lat)
    qn, qr = qn_ref[...], qr_ref[...]
    tq = qn.shape[0]
    lane = lax.broadcasted_iota(jnp.int32, qn.shape, 1)
    zero = jnp.zeros_like(qn)
    qs = []
    for hd in range(2):
        mn = (lane // B_NOPE) == hd
        mr = ((lane % (LANES // 2)) // (B_ROPE // 2)) == hd
        qs.append(jnp.concatenate([jnp.where(mn, qn, zero), jnp.where(mr, qr, zero)], axis=1))
    o = _flash(jnp.concatenate(qs, axis=0), kc_ref[...], vc_ref[...], kl_ref, vl_ref)
    o_ref[...] = jnp.where(lane < B_V, o[:tq], o[tq:]).astype(o_ref.dtype)


def _attn_diff_kernel(*refs, has_lat, out_scale):
    (q_ref, lam_ref, og_ref), kc_ref, vc_ref, kl_ref, vl_ref, o_ref = _split_attn_refs(
        refs, 3, has_lat)
    q = q_ref[...]
    tq = q.shape[0]
    lane = lax.broadcasted_iota(jnp.int32, q.shape, 1)
    zero = jnp.zeros_like(q)
    qs = [jnp.where(((lane // (D_QK_DIM // 2)) % 2) == comp, q, zero) for comp in range(2)]
    maps = _flash(jnp.concatenate(qs, axis=0), kc_ref[...], vc_ref[...], kl_ref, vl_ref)
    o = maps[:tq] - lam_ref[...] * maps[tq:]
    o_ref[...] = (_rms(o, og_ref[...]) * out_scale).astype(o_ref.dtype)


def attention(kind, srcs, o_alias, *, batch, seq, ctx_len, queries, extra=(), out_scale=1.0):
    n_lat = batch * seq
    lat = queries == "lat"
    tq = min(Q_TILE[kind], seq if lat else ctx_len)
    nq = (seq if lat else ctx_len) // tq
    q0 = 0 if lat else n_lat // tq
    c0 = n_lat // ctx_len
    rows = n_lat + batch * ctx_len

    def qspec(width, colfn):
        return pl.BlockSpec((tq, width), lambda b, g, i: (q0 + b * nq + i, colfn(g)))

    def kvspecs(kwidth, kcol, vcol):
        specs = [pl.BlockSpec((ctx_len, kwidth), lambda b, g, i: (c0 + b, kcol(g))),
                 pl.BlockSpec((ctx_len, LANES), lambda b, g, i: (c0 + b, vcol(g)))]
        if lat:
            specs += [pl.BlockSpec((seq, kwidth), lambda b, g, i: (b, kcol(g))),
                      pl.BlockSpec((seq, LANES), lambda b, g, i: (b, vcol(g)))]
        return specs

    n_kv = 4 if lat else 2
    if kind == "gqa":
        (qkv,) = srcs
        groups = C_KV_HEADS
        ow = (C_HEADS // C_KV_HEADS) * LANES
        in_specs = [qspec(ow, lambda g: g)] + kvspecs(
            LANES, lambda g: C_HEADS + g, lambda g: C_HEADS + C_KV_HEADS + g)
        args = [qkv] * (1 + n_kv)
        kern = functools.partial(_attn_gqa_kernel, has_lat=lat)
    elif kind == "mla":
        q_all, kv_all = srcs
        groups = B_HEADS // 2
        ow = LANES
        in_specs = ([qspec(LANES, lambda g: g), qspec(LANES, lambda g: groups + g)]
                    + kvspecs(2 * LANES, lambda g: g, lambda g: 2 * groups + g))
        args = [q_all, q_all] + [kv_all] * n_kv
        kern = functools.partial(_attn_mla_kernel, has_lat=lat)
    else:
        (qkv,) = srcs
        lam, onorm_g = extra
        groups = D_HEADS
        ow = LANES
        vec = pl.BlockSpec((1, LANES), lambda b, g, i: (0, 0))
        in_specs = [qspec(LANES, lambda g: g), vec, vec] + kvspecs(
            LANES, lambda g: D_HEADS + g, lambda g: 2 * D_HEADS + g)
        args = [qkv, lam, onorm_g.reshape(1, LANES)] + [qkv] * n_kv
        kern = functools.partial(_attn_diff_kernel, has_lat=lat, out_scale=out_scale)
    aliases = {}
    if not lat:
        in_specs.append(pl.BlockSpec(memory_space=pl.ANY))
        args.append(o_alias)
        aliases = {len(args) - 1: 0}
    return pl.pallas_call(
        kern,
        out_shape=jax.ShapeDtypeStruct((rows, groups * ow), BF16),
        grid=(batch, groups, nq),
        in_specs=in_specs,
        out_specs=pl.BlockSpec((tq, ow), lambda b, g, i: (q0 + b * nq + i, g)),
        input_output_aliases=aliases,
        compiler_params=_cparams("parallel", "parallel", "parallel"),
        name=f"attn_{kind}_{queries}",
    )(*args)


def _rope_tables(seq, tile, n_pairs, lane_slots):
    axis = np.asarray([0 if s is None else s[0] for s in lane_slots])
    freq = np.asarray([0 if s is None else s[1] for s in lane_slots])
    sign = np.asarray([0.0 if s is None else s[2] for s in lane_slots], np.float32)
    live = np.asarray([s is not None for s in lane_slots])
    t = jnp.arange(seq, dtype=jnp.int32)
    pos = jnp.where(jnp.asarray(axis)[None, :] == 0, (t // GRID_W)[:, None], (t % GRID_W)[:, None])
    inv = jnp.asarray(ROPE_THETA, F32) ** (-jnp.arange(0, 2 * n_pairs, 2, dtype=F32) / (2 * n_pairs))
    ang = pos.astype(F32) * inv[jnp.asarray(freq)][None, :]
    cos = jnp.where(jnp.asarray(live)[None, :], jnp.cos(ang), 1.0)
    sin = jnp.asarray(sign)[None, :] * jnp.sin(ang)
    ctab = jnp.concatenate([cos, jnp.ones((tile, LANES), F32)], axis=0)
    stab = jnp.concatenate([sin, jnp.zeros((tile, LANES), F32)], axis=0)
    return ctab, stab


def _gqa_layout():
    q = C_HEAD_DIM // 4
    perm = np.concatenate([np.arange(0, q), np.arange(2 * q, 3 * q), np.arange(q, 2 * q),
                           np.arange(3 * q, 4 * q)])
    slots = [(axis, j, sign) for sign in (-1.0, 1.0) for axis in (0, 1) for j in range(q)]
    return perm, slots


def _diff_layout():
    e = D_QK_DIM // 4
    perm, slots = [], []
    for half, sign in ((0, -1.0), (1, 1.0)):
        for comp in range(2):
            for axis in range(2):
                perm.extend(comp * D_QK_DIM + axis * 2 * e + half * e + j for j in range(e))
                slots.extend((axis, j, sign) for j in range(e))
    return np.asarray(perm), slots


def _mla_rope_layout():
    e = B_ROPE // 4
    dims, slots = [], []
    for half, sign in ((0, -1.0), (1, 1.0)):
        for hd in range(2):
            for axis in range(2):
                dims.extend((hd, axis * 2 * e + half * e + j) for j in range(e))
                slots.extend((axis, j, sign) for j in range(e))
        dims.extend([None] * (2 * B_ROPE // 2))
        slots.extend([None] * (2 * B_ROPE // 2))
    return dims, slots


def _gather_cols(w, cols):
    idx = np.asarray([0 if c is None else c for c in cols])
    keep = np.asarray([c is not None for c in cols])
    out = jnp.take(w, jnp.asarray(idx), axis=1)
    return jnp.where(jnp.asarray(keep)[None, :], out, 0).astype(BF16)


def kernel(x, c, ctx, c_ctx, mod_w, mod_b, norm_g, a_w_in, a_lb, a_onorm_g, a_w_out, b_w_in, b_qnorm_g, b_kvnorm_g, b_w_qb, b_w_kvb, b_w_out, c_w_in, c_qnorm_g, c_knorm_g, c_w_out, d_w_in, d_lambda, d_onorm_g, d_w_out, ffn_w_gu, ffn_w_down, moe_router, moe_w_gu, moe_w_down):
    batch, seq, d = x.shape
    ctx_len = ctx.shape[1]
    depth = mod_w.shape[0]
    n_lat = batch * seq
    lay_all = Layout(batch, seq, ctx_len, ROW_TILE, with_ctx=True)
    lay_lat = Layout(batch, seq, ctx_len, ROW_TILE, with_ctx=False)
    lay_proj = Layout(batch, seq, ctx_len, PROJ_TILE, with_ctx=True)

    h = jnp.concatenate([x.reshape(n_lat, d), ctx.reshape(batch * ctx_len, d)], axis=0)

    n_rows = 8 * ((batch + 1 + 7) // 8)
    cond = jnp.zeros((n_rows, d), F32).at[0].set(c_ctx).at[1:batch + 1].set(c)
    mod_all = modvec(cond, mod_w, mod_b)

    lower_bounds = jnp.cumsum(jax.nn.softmax(a_lb.astype(F32), axis=1), axis=1)

    for i in range(depth):
        last = i == depth - 1
        kind, j = i % N_MIXERS, i // N_MIXERS
        modt = mod_all[i].reshape(n_rows * N_MOD, 1, d)
        lay_out = lay_lat if last else lay_all
        g0, g1, g2, g3 = norm_g[i, 0], norm_g[i, 1], norm_g[i, 2], norm_g[i, 3]

        if kind == 0:
            z = project(h, g0, a_w_in[j].astype(BF16), lay_all, modt=modt, which=(0, 1),
                        tn=A_HEADS * LANES, name="gla_in")
            s0 = jnp.zeros((batch, A_HEADS, 2, LANES, LANES), F32)
            o, s_ctx = gla_scan(z, lower_bounds[0, i], lower_bounds[1, i], s0, None,
                                batch=batch, seg_len=ctx_len, row0=n_lat)
            o, _ = gla_scan(z, lower_bounds[0, i], lower_bounds[1, i], s_ctx, o,
                            batch=batch, seg_len=seq, row0=0)
            h = out_project(o, a_w_out[j].astype(BF16), h, g1, modt, lay_out,
                            gla_gate=(z, a_onorm_g[j], 4), name="gla_out")
        else:
            if kind == 1:
                dims, slots = _mla_rope_layout()
                rope = _rope_tables(seq, lay_proj.tile, B_ROPE // 4, slots)
                w_in = b_w_in[j]
                kr_cols = [None if dd is None else B_Q_RANK + B_KV_RANK + dd[1] for dd in dims]
                w_cat = jnp.concatenate(
                    [w_in[:, :B_Q_RANK].astype(BF16), _gather_cols(w_in, kr_cols),
                     w_in[:, B_Q_RANK:B_Q_RANK + B_KV_RANK].astype(BF16)], axis=1)
                nq_blocks = B_Q_RANK // LANES
                n_in_blocks = nq_blocks + 1 + B_KV_RANK // LANES
                blocks = [Blk(cb, rope=(cb == nq_blocks)) for cb in range(n_in_blocks)]
                cin = project(h, g0, w_cat, lay_proj, modt=modt, which=(0, 1), blocks=blocks,
                              rope=rope, name="mla_in")
                hw = B_NOPE + B_ROPE
                n_pairs = B_HEADS // 2
                qn_cols = [hd * hw + t for hd in range(B_HEADS) for t in range(B_NOPE)]
                qr_cols = [None if dd is None else (2 * p + dd[0]) * hw + B_NOPE + dd[1]
                           for p in range(n_pairs) for dd in dims]
                w_q = _gather_cols(b_w_qb[j], qn_cols + qr_cols)
                qmul = (B_NOPE + B_ROPE) ** -0.5 * LOG2E
                q_all = project(cin, b_qnorm_g[j], w_q, lay_proj, x_block=0,
                                blocks=([Blk(p, mult=qmul) for p in range(n_pairs)]
                                        + [Blk(n_pairs + p, rope=True, mult=qmul)
                                           for p in range(n_pairs)]),
                                rope=rope, out_dtype=BF16, name="mla_q")
                kw = B_NOPE + B_V
                kn_cols = [hd * kw + t for hd in range(B_HEADS) for t in range(B_NOPE)]
                v_cols = [hd * kw + B_NOPE + t for hd in range(B_HEADS) for t in range(B_V)]
                w_kv = _gather_cols(b_w_kvb[j], kn_cols + v_cols)
                kblocks = [blk for p in range(n_pairs) for blk in (Blk(p), Blk("copy"))]
                kblocks += [Blk(n_pairs + p) for p in range(n_pairs)]
                kv_all = project(cin, b_kvnorm_g[j], w_kv, lay_proj,
                                 x_block=(B_Q_RANK + LANES) // B_KV_RANK, blocks=kblocks,
                                 copy_src=(cin, nq_blocks), out_dtype=BF16, name="mla_kv")
                srcs, akind, extra, out_scale = (q_all, kv_all), "mla", (), 1.0
                w_out = b_w_out[j]
            elif kind == 2:
                perm, slots = _gqa_layout()
                rope = _rope_tables(seq, lay_proj.tile, C_HEAD_DIM // 4, slots)
                nqk = C_HEADS + C_KV_HEADS
                cols = [hd * C_HEAD_DIM + p for hd in range(nqk) for p in perm]
                cols += list(range(nqk * C_HEAD_DIM, (nqk + C_KV_HEADS) * C_HEAD_DIM))
                w_cat = _gather_cols(c_w_in[j], cols)
                gains = jnp.zeros((8, LANES), F32).at[0].set(c_qnorm_g[j][perm]).at[1].set(
                    c_knorm_g[j][perm])
                qmul = C_HEAD_DIM ** -0.5 * LOG2E
                blocks = ([Blk(hd, gain=0, rope=True, mult=qmul) for hd in range(C_HEADS)]
                          + [Blk(C_HEADS + hd, gain=1, rope=True) for hd in range(C_KV_HEADS)]
                          + [Blk(nqk + hd) for hd in range(C_KV_HEADS)])
                qkv = project(h, g0, w_cat, lay_proj, modt=modt, which=(0, 1), blocks=blocks,
                              head_gain=gains, rope=rope, out_dtype=BF16, name="gqa_in")
                srcs, akind, extra, out_scale = (qkv,), "gqa", (), 1.0
                w_out = c_w_out[j]
            else:
                perm, slots = _diff_layout()
                rope = _rope_tables(seq, lay_proj.tile, D_QK_DIM // 4, slots)
                nqk = 2 * D_HEADS
                cols = [hd * 2 * D_QK_DIM + p for hd in range(nqk) for p in perm]
                cols += list(range(nqk * 2 * D_QK_DIM, (nqk + D_HEADS) * 2 * D_QK_DIM))
                w_cat = _gather_cols(d_w_in[j], cols)
                qmul = D_QK_DIM ** -0.5 * LOG2E
                blocks = ([Blk(hd, rope=True, mult=qmul) for hd in range(D_HEADS)]
                          + [Blk(D_HEADS + hd, rope=True) for hd in range(D_HEADS)]
                          + [Blk(nqk + hd) for hd in range(D_HEADS)])
                qkv = project(h, g0, w_cat, lay_proj, modt=modt, which=(0, 1), blocks=blocks,
                              rope=rope, out_dtype=BF16, name="diff_in")
                lambda_init = 0.8 - 0.6 * math.exp(-0.3 * i)
                lam = d_lambda[j].astype(F32)
                lam_full = (jnp.exp(jnp.sum(lam[0] * lam[1])) - jnp.exp(jnp.sum(lam[2] * lam[3]))
                            + lambda_init)
                extra = (jnp.full((1, LANES), lam_full, F32), d_onorm_g[j])
                srcs, akind, out_scale = (qkv,), "diff", 1.0 - lambda_init
                w_out = d_w_out[j]
            a = attention(akind, srcs, None, batch=batch, seq=seq, ctx_len=ctx_len,
                          queries="lat", extra=extra, out_scale=out_scale)
            if not last:
                a = attention(akind, srcs, a, batch=batch, seq=seq, ctx_len=ctx_len,
                              queries="ctx", extra=extra, out_scale=out_scale)
            h = out_project(a, w_out.astype(BF16), h, g1, modt, lay_out, name=f"{akind}_out")

        if i % 2 == 0:
            h = ffn_residual(h, g2, g3, ffn_w_gu[i // 2].astype(BF16),
                             ffn_w_down[i // 2].astype(BF16), modt, lay_out)
        else:
            h = moe_residual(h, g2, g3, moe_router[i // 2], moe_w_gu[i // 2], moe_w_down[i // 2],
                             modt, lay_out)
    return h[:n_lat].reshape(batch, seq, d)
```

```python
import functools
import math
from typing import Any, NamedTuple

import jax
import jax.numpy as jnp
import numpy as np
from jax import lax
from jax.experimental import pallas as pl
from jax.experimental.pallas import tpu as pltpu

F32 = jnp.float32
BF16 = jnp.bfloat16

N_MIXERS = 4
N_MOD = 6
GRID_W = 64
ROPE_THETA = 10000.0
NORM_EPS = 1e-6
A_HEADS = 8
A_DK = 128
A_CHUNK = 32
B_HEADS = 16
B_Q_RANK = 384
B_KV_RANK = 256
B_NOPE = 64
B_ROPE = 32
B_V = 64
C_HEADS = 8
C_KV_HEADS = 2
C_HEAD_DIM = 128
D_HEADS = 8
D_QK_DIM = 64
N_EXPERTS = 8
TOP_K = 2

LANES = 128
V7X_VMEM_BYTES = 64 * 1024 * 1024
VMEM_LIMIT = V7X_VMEM_BYTES - 8 * 1024 * 1024

ROW_TILE = 1024
PROJ_TILE = 512
Q_TILE = {"gqa": 256, "mla": 512, "diff": 512}
KEY_BLOCK = 512
LOG2E = 1.4426950408889634
GLA_TILE = 256
FF_CHUNK = 512
MOE_TILE = 1024


def _cparams(*sem):
    return pltpu.CompilerParams(dimension_semantics=sem, vmem_limit_bytes=VMEM_LIMIT)


def _rms(x, g):
    return x * lax.rsqrt(jnp.mean(x * x, axis=-1, keepdims=True) + NORM_EPS) * g


def _silu(x):
    return x * jax.nn.sigmoid(x)


class Layout:
    def __init__(self, batch, seq, ctx_len, tile_cap, with_ctx=True):
        tile = math.gcd(math.gcd(seq, batch * ctx_len), tile_cap)
        self.batch, self.seq, self.ctx_len, self.tile = batch, seq, ctx_len, tile
        self.n_lat = batch * seq
        self.n_ctx = batch * ctx_len if with_ctx else 0
        self.rows = self.n_lat + self.n_ctx
        assert seq % tile == 0 and self.n_ctx % tile == 0
        self.lat_tiles = self.n_lat // tile
        self.tiles = self.rows // tile
        self.tiles_per_batch = seq // tile

    def mod_row(self, i):
        return jnp.where(i < self.lat_tiles, 1 + i // self.tiles_per_batch, 0)

    def rope_block(self, i):
        return jnp.where(i < self.lat_tiles, i % self.tiles_per_batch, self.tiles_per_batch)

    def mod_spec(self, which, d):
        return pl.BlockSpec((1, 1, d), lambda i, *_: (self.mod_row(i) * N_MOD + which, 0, 0))


def _modvec_kernel(c_ref, w_ref, b_ref, o_ref):
    cond = _silu(c_ref[...]).astype(BF16)
    o_ref[0] = jnp.dot(cond, w_ref[0].astype(BF16), preferred_element_type=F32) + b_ref[0]


def modvec(cond_rows, mod_w, mod_b):
    depth, d, n = mod_w.shape
    r = cond_rows.shape[0]
    tn = d
    return pl.pallas_call(
        _modvec_kernel,
        out_shape=jax.ShapeDtypeStruct((depth, r, n), F32),
        grid=(depth, n // tn),
        in_specs=[pl.BlockSpec((r, d), lambda l, j: (0, 0)),
                  pl.BlockSpec((1, d, tn), lambda l, j: (l, 0, j)),
                  pl.BlockSpec((1, 1, tn), lambda l, j: (l, 0, j))],
        out_specs=pl.BlockSpec((1, r, tn), lambda l, j: (l, 0, j)),
        compiler_params=_cparams("parallel", "parallel"),
        name="modvec",
    )(cond_rows, mod_w, mod_b.reshape(depth, 1, n))


class Blk(NamedTuple):
    src: Any
    gain: Any = None
    rope: bool = False
    mult: Any = None


def _proj_kernel(*refs, has_mod, blocks, vt_srcs, has_gain, has_rope, has_copy):
    it = iter(refs)
    x_ref, g_ref = next(it), next(it)
    sh_ref = sc_ref = hg_ref = c_ref = s_ref = cp_ref = vt_ref = None
    if has_mod:
        sh_ref, sc_ref = next(it), next(it)
    w_ref = next(it)
    if has_gain:
        hg_ref = next(it)
    if has_rope:
        c_ref, s_ref = next(it), next(it)
    if has_copy:
        cp_ref = next(it)
    o_ref = next(it)
    if vt_srcs:
        vt_ref = next(it)
    u_ref = next(it)

    @pl.when(pl.program_id(1) == 0)
    def _():
        y = _rms(x_ref[...].astype(F32), g_ref[...])
        if has_mod:
            y = y * (1 + sc_ref[0]) + sh_ref[0]
        u_ref[...] = y.astype(BF16)

    acc = jnp.dot(u_ref[...], w_ref[...], preferred_element_type=F32)
    if blocks is None:
        o_ref[...] = acc.astype(o_ref.dtype)
        return
    for c, blk in enumerate(blocks):
        if blk.src == "copy":
            a = cp_ref[...].astype(F32)
        else:
            a = acc[:, blk.src * LANES:(blk.src + 1) * LANES]
        if blk.gain is not None:
            a = _rms(a, hg_ref[blk.gain:blk.gain + 1, :])
        if blk.rope:
            a = a * c_ref[...] + pltpu.roll(a, LANES // 2, 1) * s_ref[...]
        if blk.mult is not None:
            a = a * blk.mult
        o_ref[:, c * LANES:(c + 1) * LANES] = a.astype(o_ref.dtype)
    for c, src in enumerate(vt_srcs):
        a = acc[:, src * LANES:(src + 1) * LANES]
        vt_ref[c * LANES:(c + 1) * LANES, :] = a.T.astype(vt_ref.dtype)


def project(x, g, w, lay, *, modt=None, which=None, x_block=0, tn=None, blocks=None, vt_srcs=(),
            head_gain=None, rope=None, copy_src=None, out_dtype=F32, name="project"):
    k, n = w.shape
    tm = lay.tile
    tn = n if tn is None else tn
    assert blocks is None or tn == n
    n_out = n if blocks is None else len(blocks) * LANES
    in_specs = [pl.BlockSpec((tm, k), lambda i, j: (i, x_block)),
                pl.BlockSpec((1, k), lambda i, j: (0, 0))]
    args = [x, g.reshape(1, k)]
    if modt is not None:
        in_specs += [lay.mod_spec(which[0], k), lay.mod_spec(which[1], k)]
        args += [modt, modt]
    in_specs.append(pl.BlockSpec((k, tn), lambda i, j: (0, j)))
    args.append(w)
    if head_gain is not None:
        in_specs.append(pl.BlockSpec(head_gain.shape, lambda i, j: (0, 0)))
        args.append(head_gain)
    if rope is not None:
        spec = pl.BlockSpec((tm, LANES), lambda i, j: (lay.rope_block(i), 0))
        in_specs += [spec, spec]
        args += list(rope)
    if copy_src is not None:
        cp_arr, cp_block = copy_src
        in_specs.append(pl.BlockSpec((tm, LANES), lambda i, j: (i, cp_block)))
        args.append(cp_arr)
    out_shape = [jax.ShapeDtypeStruct((lay.rows, n_out), out_dtype)]
    out_specs = [pl.BlockSpec((tm, n_out if blocks is not None else tn), lambda i, j: (i, j))]
    if vt_srcs:
        out_shape.append(jax.ShapeDtypeStruct((len(vt_srcs) * LANES, lay.rows), BF16))
        out_specs.append(pl.BlockSpec((len(vt_srcs) * LANES, tm), lambda i, j: (0, i)))
    kern = functools.partial(_proj_kernel, has_mod=modt is not None, blocks=blocks,
                             vt_srcs=tuple(vt_srcs), has_gain=head_gain is not None,
                             has_rope=rope is not None, has_copy=copy_src is not None)
    outs = pl.pallas_call(
        kern,
        out_shape=out_shape,
        grid=(lay.tiles, n // tn),
        in_specs=in_specs,
        out_specs=out_specs,
        scratch_shapes=[pltpu.VMEM((tm, k), BF16)],
        compiler_params=_cparams("parallel", "arbitrary"),
        name=name,
    )(*args)
    return outs if vt_srcs else outs[0]


def _oproj_kernel(*refs, lat_tiles, has_ctx, is_gla):
    it = iter(refs)
    a = next(it)[...]
    if has_ctx:
        a_ctx = next(it)[...]
        a = jnp.where(pl.program_id(0) < lat_tiles, a, a_ctx)
    if is_gla:
        gz, og = next(it)[...], next(it)[...]
        parts = []
        for hd in range(A_HEADS):
            sl = slice(hd * LANES, (hd + 1) * LANES)
            parts.append((_rms(a[:, sl], og) * _silu(gz[:, sl])).astype(BF16))
        a = jnp.concatenate(parts, axis=1)
    w_ref, h_ref, g_ref, gate_ref, o_ref = it
    y = jnp.dot(a, w_ref[...], preferred_element_type=F32)
    o_ref[...] = h_ref[...] + gate_ref[0] * _rms(y, g_ref[...])


def out_project(a, w, h, g, modt, lay, *, gla_gate=None, name="out_project"):
    k, d = w.shape
    tm = lay.tile
    row = lambda i: (i, 0)
    a_lat, a_ctx = a
    has_ctx = a_ctx is not None and lay.n_ctx > 0
    last_lat = lay.lat_tiles - 1
    in_specs = [pl.BlockSpec((tm, k), lambda i: (jnp.minimum(i, last_lat), 0))]
    args = [a_lat]
    if has_ctx:
        in_specs.append(pl.BlockSpec((tm, k), lambda i: (jnp.maximum(i - lay.lat_tiles, 0), 0)))
        args.append(a_ctx)
    if gla_gate is not None:
        z, onorm_g, gate_block = gla_gate
        in_specs += [pl.BlockSpec((tm, k), lambda i: (i, gate_block)),
                     pl.BlockSpec((1, LANES), lambda i: (0, 0))]
        args += [z, onorm_g.reshape(1, LANES)]
    in_specs += [pl.BlockSpec((k, d), lambda i: (0, 0)),
                 pl.BlockSpec((tm, d), row),
                 pl.BlockSpec((1, d), lambda i: (0, 0)),
                 lay.mod_spec(2, d)]
    args += [w, h, g.reshape(1, d), modt]
    kern = functools.partial(_oproj_kernel, lat_tiles=lay.lat_tiles, has_ctx=has_ctx,
                             is_gla=gla_gate is not None)
    return pl.pallas_call(
        kern,
        out_shape=jax.ShapeDtypeStruct((lay.rows, d), F32),
        grid=(lay.tiles,),
        in_specs=in_specs,
        out_specs=pl.BlockSpec((tm, d), row),
        compiler_params=_cparams("parallel"),
        name=name,
    )(*args)


def _ffn_kernel(h_ref, g2_ref, sh_ref, sc_ref, wg_ref, wu_ref, wd_ref, g3_ref, gate_ref, o_ref,
                u_ref, acc_ref):
    j = pl.program_id(1)

    @pl.when(j == 0)
    def _():
        y = _rms(h_ref[...], g2_ref[...]) * (1 + sc_ref[0]) + sh_ref[0]
        u_ref[...] = y.astype(BF16)
        acc_ref[...] = jnp.zeros_like(acc_ref)

    u = u_ref[...]
    a = jnp.dot(u, wg_ref[...], preferred_element_type=F32)
    b = jnp.dot(u, wu_ref[...], preferred_element_type=F32)
    hid = (_silu(a) * b).astype(BF16)
    acc_ref[...] += jnp.dot(hid, wd_ref[...], preferred_element_type=F32)

    @pl.when(j == pl.num_programs(1) - 1)
    def _():
        o_ref[...] = h_ref[...] + gate_ref[0] * _rms(acc_ref[...], g3_ref[...])


def ffn_residual(h, g2, g3, w_gu, w_down, modt, lay):
    d = h.shape[1]
    ff = w_down.shape[0]
    fc = FF_CHUNK
    nfc = ff // fc
    tm = lay.tile
    row = lambda i, j: (i, 0)
    vec = pl.BlockSpec((1, d), lambda i, j: (0, 0))
    return pl.pallas_call(
        _ffn_kernel,
        out_shape=jax.ShapeDtypeStruct((lay.rows, d), F32),
        grid=(lay.tiles, nfc),
        in_specs=[pl.BlockSpec((tm, d), row), vec, lay.mod_spec(3, d), lay.mod_spec(4, d),
                  pl.BlockSpec((d, fc), lambda i, j: (0, j)),
                  pl.BlockSpec((d, fc), lambda i, j: (0, j + nfc)),
                  pl.BlockSpec((fc, d), lambda i, j: (j, 0)),
                  vec, lay.mod_spec(5, d)],
        out_specs=pl.BlockSpec((tm, d), row),
        scratch_shapes=[pltpu.VMEM((tm, d), BF16), pltpu.VMEM((tm, d), F32)],
        compiler_params=_cparams("parallel", "arbitrary"),
        name="ffn_residual",
    )(h, g2.reshape(1, d), modt, modt, w_gu, w_gu, w_down, g3.reshape(1, d), modt)


def _router_kernel(h_ref, g2_ref, sh_ref, sc_ref, r_ref, u_ref, ids_ref, wts_ref):
    u = (_rms(h_ref[...], g2_ref[...]) * (1 + sc_ref[0]) + sh_ref[0]).astype(BF16)
    u_ref[...] = u
    logits = jnp.dot(u, r_ref[...], preferred_element_type=F32)
    lane = lax.broadcasted_iota(jnp.int32, logits.shape, 1)
    neg = jnp.float32(-jnp.inf)
    logits = jnp.where(lane < N_EXPERTS, logits, neg)
    m1 = jnp.max(logits, axis=-1, keepdims=True)
    i1 = jnp.min(jnp.where(logits == m1, lane, LANES), axis=-1, keepdims=True)
    rest = jnp.where(lane == i1, neg, logits)
    m2 = jnp.max(rest, axis=-1, keepdims=True)
    i2 = jnp.min(jnp.where(rest == m2, lane, LANES), axis=-1, keepdims=True)
    e = jnp.exp(m2 - m1)
    w1 = 1.0 / (1.0 + e)
    w2 = e / (1.0 + e)
    ids_ref[...] = jnp.where(lane == 0, i1, jnp.where(lane == 1, i2, 0))
    wts_ref[...] = jnp.where(lane == 0, w1, jnp.where(lane == 1, w2, 0.0))


def moe_router(h, g2, router, modt, lay):
    d = h.shape[1]
    tm = lay.tile
    r_pad = jnp.zeros((d, LANES), BF16).at[:, :N_EXPERTS].set(router.astype(BF16))
    row = lambda i: (i, 0)
    return pl.pallas_call(
        _router_kernel,
        out_shape=(jax.ShapeDtypeStruct((lay.rows, d), BF16),
                   jax.ShapeDtypeStruct((lay.rows, LANES), jnp.int32),
                   jax.ShapeDtypeStruct((lay.rows, LANES), F32)),
        grid=(lay.tiles,),
        in_specs=[pl.BlockSpec((tm, d), row), pl.BlockSpec((1, d), lambda i: (0, 0)),
                  lay.mod_spec(3, d), lay.mod_spec(4, d),
                  pl.BlockSpec((d, LANES), lambda i: (0, 0))],
        out_specs=(pl.BlockSpec((tm, d), row), pl.BlockSpec((tm, LANES), row),
                   pl.BlockSpec((tm, LANES), row)),
        compiler_params=_cparams("parallel"),
        name="moe_router",
    )(h, g2.reshape(1, d), modt, modt, r_pad)


def _moe_ffn_kernel(te_ref, nu_ref, x_ref, wg_ref, wu_ref, wd_ref, o_ref, acc_ref):
    t, j = pl.program_id(0), pl.program_id(1)

    @pl.when(j == 0)
    def _():
        acc_ref[...] = jnp.zeros_like(acc_ref)

    @pl.when(t < nu_ref[0])
    def _():
        x = x_ref[...]
        a = jnp.dot(x, wg_ref[0, 0].astype(BF16), preferred_element_type=F32)
        b = jnp.dot(x, wu_ref[0, 0].astype(BF16), preferred_element_type=F32)
        hid = (_silu(a) * b).astype(BF16)
        acc_ref[...] += jnp.dot(hid, wd_ref[0, 0].astype(BF16), preferred_element_type=F32)

    @pl.when(j == pl.num_programs(1) - 1)
    def _():
        o_ref[...] = acc_ref[...].astype(o_ref.dtype)


def moe_ffn(x_sorted, tile_expert, n_used, w_gu, w_down, layer):
    n_slots, d = x_sorted.shape
    ff = w_down.shape[2]
    fc = FF_CHUNK
    nfc = ff // fc
    tm = MOE_TILE
    n_tiles = n_slots // tm

    def jj(t, j, te, nu):
        return jnp.where(t < nu[0], j, nfc - 1)

    grid_spec = pltpu.PrefetchScalarGridSpec(
        num_scalar_prefetch=2,
        grid=(n_tiles, nfc),
        in_specs=[pl.BlockSpec((tm, d), lambda t, j, te, nu: (t, 0)),
                  pl.BlockSpec((1, 1, d, fc),
                               lambda t, j, te, nu: (layer, te[t], 0, jj(t, j, te, nu))),
                  pl.BlockSpec((1, 1, d, fc),
                               lambda t, j, te, nu: (layer, te[t], 0, jj(t, j, te, nu) + nfc)),
                  pl.BlockSpec((1, 1, fc, d),
                               lambda t, j, te, nu: (layer, te[t], jj(t, j, te, nu), 0))],
        out_specs=pl.BlockSpec((tm, d), lambda t, j, te, nu: (t, 0)),
        scratch_shapes=[pltpu.VMEM((tm, d), F32)],
    )
    return pl.pallas_call(
        _moe_ffn_kernel,
        out_shape=jax.ShapeDtypeStruct((n_slots, d), BF16),
        grid_spec=grid_spec,
        compiler_params=_cparams("arbitrary", "arbitrary"),
        name="moe_ffn",
    )(tile_expert, n_used, x_sorted, w_gu, w_gu, w_down)


def _combine_kernel(y_ref, w_ref, h_ref, g_ref, gate_ref, o_ref):
    w = w_ref[...]
    y = w[:, 0:1] * y_ref[0].astype(F32) + w[:, 1:2] * y_ref[1].astype(F32)
    o_ref[...] = h_ref[...] + gate_ref[0] * _rms(y, g_ref[...])


def combine_residual(y_pair, wts, h, g3, modt, lay):
    d = h.shape[1]
    tm = lay.tile
    row = lambda i: (i, 0)
    return pl.pallas_call(
        _combine_kernel,
        out_shape=jax.ShapeDtypeStruct((lay.rows, d), F32),
        grid=(lay.tiles,),
        in_specs=[pl.BlockSpec((2, tm, d), lambda i: (0, i, 0)), pl.BlockSpec((tm, LANES), row),
                  pl.BlockSpec((tm, d), row), pl.BlockSpec((1, d), lambda i: (0, 0)),
                  lay.mod_spec(5, d)],
        out_specs=pl.BlockSpec((tm, d), row),
        compiler_params=_cparams("parallel"),
        name="moe_combine",
    )(y_pair, wts, h, g3.reshape(1, d), modt)


def moe_residual(h, g2, g3, router, w_gu, w_down, layer, modt, lay):
    rows = lay.rows
    u, ids, wts = moe_router(h, g2, router, modt, lay)
    e_flat = ids[:, :TOP_K].reshape(-1)
    n = rows * TOP_K
    onehot = (e_flat[:, None] == jnp.arange(N_EXPERTS, dtype=jnp.int32)[None, :]).astype(jnp.int32)
    csum = jnp.cumsum(onehot, axis=0)
    counts = csum[-1]
    rank = jnp.sum(csum * onehot, axis=1) - 1
    padded = ((counts + MOE_TILE - 1) // MOE_TILE) * MOE_TILE
    ends = jnp.cumsum(padded)
    starts = ends - padded
    dest = jnp.sum(starts[None, :] * onehot, axis=1) + rank
    n_slots = ((n + N_EXPERTS * (MOE_TILE - 1)) // MOE_TILE + 1) * MOE_TILE
    n_tiles = n_slots // MOE_TILE
    src_tok = jnp.zeros((n_slots,), jnp.int32).at[dest].set(
        jnp.arange(n, dtype=jnp.int32) // TOP_K, unique_indices=True)
    n_used = (ends[-1] // MOE_TILE).astype(jnp.int32).reshape(1)
    tile_start = jnp.arange(n_tiles, dtype=jnp.int32) * MOE_TILE
    tile_expert = jnp.minimum(
        jnp.sum((ends[None, :] <= tile_start[:, None]).astype(jnp.int32), axis=1), N_EXPERTS - 1)
    last_e = jnp.max(jnp.where(counts > 0, jnp.arange(N_EXPERTS, dtype=jnp.int32), 0))
    tile_expert = jnp.where(tile_start < ends[-1], tile_expert, last_e)
    x_sorted = u.at[src_tok].get(mode="promise_in_bounds")
    y_sorted = moe_ffn(x_sorted, tile_expert, n_used, w_gu, w_down, layer)
    y_pair = y_sorted.at[dest.reshape(rows, TOP_K).T].get(mode="promise_in_bounds")
    return combine_residual(y_pair, wts, h, g3, modt, lay)


def _gla_kernel(q_ref, ff_ref, fb_ref, v_ref, lbf_ref, lbb_ref, s0_ref, o_ref, st_ref,
                s_scr, ob_scr, *, n_tiles):
    rt = GLA_TILE
    nch = rt // A_CHUNK
    s_scr[...] = s0_ref[0, 0]
    row = lax.broadcasted_iota(jnp.int32, (rt, LANES), 0) % A_CHUNK
    ci = lax.broadcasted_iota(jnp.int32, (rt, rt), 0)
    si = lax.broadcasted_iota(jnp.int32, (rt, rt), 1)
    same_chunk = (ci // A_CHUNK) == (si // A_CHUNK)
    nt = (((1,), (1,)), ((), ()))

    def direction(r0, f_ref, lb_ref, d, reverse):
        rows = pl.ds(pl.multiple_of(r0, rt), rt)
        lb = lb_ref[...]
        f = lb + (1.0 - lb) * jax.nn.sigmoid(f_ref[rows, :])
        logf = jnp.log(f)
        kk = 1.0 - f
        q = q_ref[rows, :] * (A_DK ** -0.5)
        v = v_ref[rows, :]
        cum = logf
        for s in (1, 2, 4, 8, 16):
            if reverse:
                cum = cum + jnp.where(row < A_CHUNK - s, pltpu.roll(cum, rt - s, 0), 0.0)
            else:
                cum = cum + jnp.where(row >= s, pltpu.roll(cum, s, 0), 0.0)
        cum3 = cum.reshape(nch, A_CHUNK, LANES)
        end3 = cum3[:, 0:1, :] if reverse else cum3[:, A_CHUNK - 1:A_CHUNK, :]
        q_dec = (q * jnp.exp(cum)).astype(BF16)
        k_neg = (kk * jnp.exp(-cum)).astype(BF16)
        k_end = (kk.reshape(nch, A_CHUNK, LANES) * jnp.exp(end3 - cum3)).astype(BF16)
        vb = v.astype(BF16)
        decay = jnp.exp(end3)
        att = lax.dot_general(q_dec, k_neg, nt, preferred_element_type=F32)
        keep = same_chunk & ((si >= ci) if reverse else (si <= ci))
        o_intra = jnp.dot(jnp.where(keep, att, 0.0).astype(BF16), vb, preferred_element_type=F32)
        q_dec3 = q_dec.reshape(nch, A_CHUNK, LANES)
        v3 = vb.reshape(nch, A_CHUNK, LANES)
        upds = [lax.dot_general(v3[g], k_end[g], (((0,), (0,)), ((), ())),
                                preferred_element_type=F32) for g in range(nch)]
        before = [None] * nch
        state = s_scr[d]
        for g in (range(nch - 1, -1, -1) if reverse else range(nch)):
            before[g] = state.astype(BF16)
            state = decay[g] * state + upds[g]
        s_scr[d] = state
        outs = [lax.dot_general(q_dec3[g], before[g], nt, preferred_element_type=F32)
                for g in range(nch)]
        return o_intra + jnp.concatenate(outs, axis=0), rows

    def body(t, carry):
        o_f, rows_f = direction(t * rt, ff_ref, lbf_ref, 0, False)
        o_ref[rows_f, :] = o_f
        o_b, rows_b = direction((n_tiles - 1 - t) * rt, fb_ref, lbb_ref, 1, True)
        ob_scr[rows_b, :] = o_b
        return carry

    lax.fori_loop(0, n_tiles, body, 0, unroll=2 if n_tiles % 2 == 0 else 1)
    o_ref[...] += ob_scr[...]
    st_ref[0, 0] = s_scr[...]


def gla_scan(z, lb_f, lb_b, s0, *, batch, seg_len, row0):
    n_tiles = seg_len // GLA_TILE
    rb0 = row0 // seg_len
    assert row0 % seg_len == 0

    def col(kind):
        return pl.BlockSpec((seg_len, LANES), lambda b, h: (rb0 + b, kind * A_HEADS + h))

    lbspec = pl.BlockSpec((1, LANES), lambda b, h: (0, h))
    sspec = pl.BlockSpec((1, 1, 2, LANES, LANES), lambda b, h: (b, h, 0, 0, 0))
    kern = functools.partial(_gla_kernel, n_tiles=n_tiles)
    return pl.pallas_call(
        kern,
        out_shape=(jax.ShapeDtypeStruct((batch * seg_len, A_HEADS * LANES), F32),
                   jax.ShapeDtypeStruct(s0.shape, F32)),
        grid=(batch, A_HEADS),
        in_specs=[col(0), col(1), col(2), col(3), lbspec, lbspec, sspec],
        out_specs=(pl.BlockSpec((seg_len, LANES), lambda b, h: (b, h)), sspec),
        scratch_shapes=[pltpu.VMEM((2, LANES, LANES), F32), pltpu.VMEM((seg_len, LANES), F32)],
        compiler_params=_cparams("parallel", "parallel"),
        name="gla_scan",
    )(z, z, z, z, lb_f.reshape(1, -1), lb_b.reshape(1, -1), s0)


def _flash(q, kc, vc, kl_ref, vl_ref):
    nt = (((1,), (1,)), ((), ()))
    n_kblocks = 0 if kl_ref is None else kl_ref.shape[0] // KEY_BLOCK
    m = l = acc = None
    for blk in range(n_kblocks + 1):
        if blk == 0:
            kb, vb = kc, vc
        else:
            ks = slice((blk - 1) * KEY_BLOCK, blk * KEY_BLOCK)
            kb, vb = kl_ref[ks, :], vl_ref[ks, :]
        s = lax.dot_general(q, kb, nt, preferred_element_type=F32)
        m_blk = jnp.max(s, axis=-1, keepdims=True)
        m_new = m_blk if blk == 0 else jnp.maximum(m, m_blk)
        p = jnp.exp2(s - m_new)
        psum = p[:, :LANES]
        for t in range(1, p.shape[1] // LANES):
            psum = psum + p[:, t * LANES:(t + 1) * LANES]
        pv = jnp.dot(p.astype(BF16), vb, preferred_element_type=F32)
        if blk == 0:
            l, acc = psum, pv
        else:
            alpha = jnp.exp2(m - m_new)
            l = alpha * l + psum
            acc = alpha * acc + pv
        m = m_new
    return acc / jnp.sum(l, axis=-1, keepdims=True)


def _split_attn_refs(refs, n_lead, has_lat):
    lead = refs[:n_lead]
    kc_ref, vc_ref = refs[n_lead], refs[n_lead + 1]
    if has_lat:
        kl_ref, vl_ref, o_ref = refs[n_lead + 2:]
    else:
        kl_ref = vl_ref = None
        (o_ref,) = refs[n_lead + 2:]
    return lead, kc_ref, vc_ref, kl_ref, vl_ref, o_ref


def _attn_gqa_kernel(*refs, has_lat):
    (q_ref,), kc_ref, vc_ref, kl_ref, vl_ref, o_ref = _split_attn_refs(refs, 1, has_lat)
    group = C_HEADS // C_KV_HEADS
    tq = q_ref.shape[0]
    q = jnp.concatenate([q_ref[:, g * LANES:(g + 1) * LANES] for g in range(group)], axis=0)
    o = _flash(q, kc_ref[...], vc_ref[...], kl_ref, vl_ref)
    for g in range(group):
        o_ref[:, g * LANES:(g + 1) * LANES] = o[g * tq:(g + 1) * tq].astype(o_ref.dtype)


def _attn_mla_kernel(*refs, has_lat):
    (qn_ref, qr_ref), kc_ref, vc_ref, kl_ref, vl_ref, o_ref = _split_attn_refs(refs, 2, has_lat)
    qn, qr = qn_ref[...], qr_ref[...]
    tq = qn.shape[0]
    lane = lax.broadcasted_iota(jnp.int32, qn.shape, 1)
    zero = jnp.zeros_like(qn)
    qs = []
    for hd in range(2):
        mn = (lane // B_NOPE) == hd
        mr = ((lane % (LANES // 2)) // (B_ROPE // 2)) == hd
        qs.append(jnp.concatenate([jnp.where(mn, qn, zero), jnp.where(mr, qr, zero)], axis=1))
    o = _flash(jnp.concatenate(qs, axis=0), kc_ref[...], vc_ref[...], kl_ref, vl_ref)
    o_ref[...] = jnp.where(lane < B_V, o[:tq], o[tq:]).astype(o_ref.dtype)


def _attn_diff_kernel(*refs, has_lat, out_scale):
    (q_ref, lam_ref, og_ref), kc_ref, vc_ref, kl_ref, vl_ref, o_ref = _split_attn_refs(
        refs, 3, has_lat)
    q = q_ref[...]
    tq = q.shape[0]
    lane = lax.broadcasted_iota(jnp.int32, q.shape, 1)
    zero = jnp.zeros_like(q)
    qs = [jnp.where(((lane // (D_QK_DIM // 2)) % 2) == comp, q, zero) for comp in range(2)]
    maps = _flash(jnp.concatenate(qs, axis=0), kc_ref[...], vc_ref[...], kl_ref, vl_ref)
    o = maps[:tq] - lam_ref[...] * maps[tq:]
    o_ref[...] = (_rms(o, og_ref[...]) * out_scale).astype(o_ref.dtype)


def attention(kind, srcs, *, batch, seq, ctx_len, queries, extra=(), out_scale=1.0):
    n_lat = batch * seq
    lat = queries == "lat"
    tq = min(Q_TILE[kind], seq if lat else ctx_len)
    nq = (seq if lat else ctx_len) // tq
    q0 = 0 if lat else n_lat // tq
    c0 = n_lat // ctx_len
    rows = batch * (seq if lat else ctx_len)

    def qspec(width, colfn):
        return pl.BlockSpec((tq, width), lambda b, g, i: (q0 + b * nq + i, colfn(g)))

    def kvspecs(kwidth, kcol, vcol):
        specs = [pl.BlockSpec((ctx_len, kwidth), lambda b, g, i: (c0 + b, kcol(g))),
                 pl.BlockSpec((ctx_len, LANES), lambda b, g, i: (c0 + b, vcol(g)))]
        if lat:
            specs += [pl.BlockSpec((seq, kwidth), lambda b, g, i: (b, kcol(g))),
                      pl.BlockSpec((seq, LANES), lambda b, g, i: (b, vcol(g)))]
        return specs

    n_kv = 4 if lat else 2
    if kind == "gqa":
        (qkv,) = srcs
        groups = C_KV_HEADS
        ow = (C_HEADS // C_KV_HEADS) * LANES
        in_specs = [qspec(ow, lambda g: g)] + kvspecs(
            LANES, lambda g: C_HEADS + g, lambda g: C_HEADS + C_KV_HEADS + g)
        args = [qkv] * (1 + n_kv)
        kern = functools.partial(_attn_gqa_kernel, has_lat=lat)
    elif kind == "mla":
        q_all, kv_all = srcs
        groups = B_HEADS // 2
        ow = LANES
        in_specs = ([qspec(LANES, lambda g: g), qspec(LANES, lambda g: groups + g)]
                    + kvspecs(2 * LANES, lambda g: g, lambda g: 2 * groups + g))
        args = [q_all, q_all] + [kv_all] * n_kv
        kern = functools.partial(_attn_mla_kernel, has_lat=lat)
    else:
        (qkv,) = srcs
        lam, onorm_g = extra
        groups = D_HEADS
        ow = LANES
        vec = pl.BlockSpec((1, LANES), lambda b, g, i: (0, 0))
        in_specs = [qspec(LANES, lambda g: g), vec, vec] + kvspecs(
            LANES, lambda g: D_HEADS + g, lambda g: 2 * D_HEADS + g)
        args = [qkv, lam, onorm_g.reshape(1, LANES)] + [qkv] * n_kv
        kern = functools.partial(_attn_diff_kernel, has_lat=lat, out_scale=out_scale)
    return pl.pallas_call(
        kern,
        out_shape=jax.ShapeDtypeStruct((rows, groups * ow), BF16),
        grid=(batch, groups, nq),
        in_specs=in_specs,
        out_specs=pl.BlockSpec((tq, ow), lambda b, g, i: (b * nq + i, g)),
        compiler_params=_cparams("parallel", "parallel", "parallel"),
        name=f"attn_{kind}_{queries}",
    )(*args)


def _rope_tables(seq, tile, n_pairs, lane_slots):
    axis = np.asarray([0 if s is None else s[0] for s in lane_slots])
    freq = np.asarray([0 if s is None else s[1] for s in lane_slots])
    sign = np.asarray([0.0 if s is None else s[2] for s in lane_slots], np.float32)
    live = np.asarray([s is not None for s in lane_slots])
    t = jnp.arange(seq, dtype=jnp.int32)
    pos = jnp.where(jnp.asarray(axis)[None, :] == 0, (t // GRID_W)[:, None], (t % GRID_W)[:, None])
    inv = jnp.asarray(ROPE_THETA, F32) ** (-jnp.arange(0, 2 * n_pairs, 2, dtype=F32) / (2 * n_pairs))
    ang = pos.astype(F32) * inv[jnp.asarray(freq)][None, :]
    cos = jnp.where(jnp.asarray(live)[None, :], jnp.cos(ang), 1.0)
    sin = jnp.asarray(sign)[None, :] * jnp.sin(ang)
    ctab = jnp.concatenate([cos, jnp.ones((tile, LANES), F32)], axis=0)
    stab = jnp.concatenate([sin, jnp.zeros((tile, LANES), F32)], axis=0)
    return ctab, stab


def _gqa_layout():
    q = C_HEAD_DIM // 4
    perm = np.concatenate([np.arange(0, q), np.arange(2 * q, 3 * q), np.arange(q, 2 * q),
                           np.arange(3 * q, 4 * q)])
    slots = [(axis, j, sign) for sign in (-1.0, 1.0) for axis in (0, 1) for j in range(q)]
    return perm, slots


def _diff_layout():
    e = D_QK_DIM // 4
    perm, slots = [], []
    for half, sign in ((0, -1.0), (1, 1.0)):
        for comp in range(2):
            for axis in range(2):
                perm.extend(comp * D_QK_DIM + axis * 2 * e + half * e + j for j in range(e))
                slots.extend((axis, j, sign) for j in range(e))
    return np.asarray(perm), slots


def _mla_rope_layout():
    e = B_ROPE // 4
    dims, slots = [], []
    for half, sign in ((0, -1.0), (1, 1.0)):
        for hd in range(2):
            for axis in range(2):
                dims.extend((hd, axis * 2 * e + half * e + j) for j in range(e))
                slots.extend((axis, j, sign) for j in range(e))
        dims.extend([None] * (2 * B_ROPE // 2))
        slots.extend([None] * (2 * B_ROPE // 2))
    return dims, slots


def _gather_cols(w, cols):
    idx = np.asarray([0 if c is None else c for c in cols])
    keep = np.asarray([c is not None for c in cols])
    out = jnp.take(w, jnp.asarray(idx), axis=1)
    return jnp.where(jnp.asarray(keep)[None, :], out, 0).astype(BF16)


def kernel(x, c, ctx, c_ctx, mod_w, mod_b, norm_g, a_w_in, a_lb, a_onorm_g, a_w_out, b_w_in, b_qnorm_g, b_kvnorm_g, b_w_qb, b_w_kvb, b_w_out, c_w_in, c_qnorm_g, c_knorm_g, c_w_out, d_w_in, d_lambda, d_onorm_g, d_w_out, ffn_w_gu, ffn_w_down, moe_router, moe_w_gu, moe_w_down):
    batch, seq, d = x.shape
    ctx_len = ctx.shape[1]
    depth = mod_w.shape[0]
    n_lat = batch * seq
    lay_all = Layout(batch, seq, ctx_len, ROW_TILE, with_ctx=True)
    lay_lat = Layout(batch, seq, ctx_len, ROW_TILE, with_ctx=False)
    lay_proj = Layout(batch, seq, ctx_len, PROJ_TILE, with_ctx=True)

    h = jnp.concatenate([x.reshape(n_lat, d), ctx.reshape(batch * ctx_len, d)], axis=0)

    n_rows = 8 * ((batch + 1 + 7) // 8)
    cond = jnp.zeros((n_rows, d), F32).at[0].set(c_ctx).at[1:batch + 1].set(c)
    mod_all = modvec(cond, mod_w, mod_b)

    lower_bounds = jnp.cumsum(jax.nn.softmax(a_lb.astype(F32), axis=1), axis=1)

    for i in range(depth):
        last = i == depth - 1
        kind, j = i % N_MIXERS, i // N_MIXERS
        modt = mod_all[i].reshape(n_rows * N_MOD, 1, d)
        lay_out = lay_lat if last else lay_all
        g0, g1, g2, g3 = norm_g[i, 0], norm_g[i, 1], norm_g[i, 2], norm_g[i, 3]

        if kind == 0:
            z = project(h, g0, a_w_in[j].astype(BF16), lay_all, modt=modt, which=(0, 1),
                        tn=A_HEADS * LANES, name="gla_in")
            s0 = jnp.zeros((batch, A_HEADS, 2, LANES, LANES), F32)
            o_ctx, s_ctx = gla_scan(z, lower_bounds[0, i], lower_bounds[1, i], s0,
                                    batch=batch, seg_len=ctx_len, row0=n_lat)
            o_lat, _ = gla_scan(z, lower_bounds[0, i], lower_bounds[1, i], s_ctx,
                                batch=batch, seg_len=seq, row0=0)
            h = out_project((o_lat, o_ctx), a_w_out[j].astype(BF16), h, g1, modt, lay_out,
                            gla_gate=(z, a_onorm_g[j], 4), name="gla_out")
        else:
            if kind == 1:
                dims, slots = _mla_rope_layout()
                rope = _rope_tables(seq, lay_proj.tile, B_ROPE // 4, slots)
                w_in = b_w_in[j]
                kr_cols = [None if dd is None else B_Q_RANK + B_KV_RANK + dd[1] for dd in dims]
                w_cat = jnp.concatenate(
                    [w_in[:, :B_Q_RANK].astype(BF16), _gather_cols(w_in, kr_cols),
                     w_in[:, B_Q_RANK:B_Q_RANK + B_KV_RANK].astype(BF16)], axis=1)
                nq_blocks = B_Q_RANK // LANES
                n_in_blocks = nq_blocks + 1 + B_KV_RANK // LANES
                blocks = [Blk(cb, rope=(cb == nq_blocks)) for cb in range(n_in_blocks)]
                cin = project(h, g0, w_cat, lay_proj, modt=modt, which=(0, 1), blocks=blocks,
                              rope=rope, name="mla_in")
                hw = B_NOPE + B_ROPE
                n_pairs = B_HEADS // 2
                qn_cols = [hd * hw + t for hd in range(B_HEADS) for t in range(B_NOPE)]
                qr_cols = [None if dd is None else (2 * p + dd[0]) * hw + B_NOPE + dd[1]
                           for p in range(n_pairs) for dd in dims]
                w_q = _gather_cols(b_w_qb[j], qn_cols + qr_cols)
                qmul = (B_NOPE + B_ROPE) ** -0.5 * LOG2E
                q_all = project(cin, b_qnorm_g[j], w_q, lay_proj, x_block=0,
                                blocks=([Blk(p, mult=qmul) for p in range(n_pairs)]
                                        + [Blk(n_pairs + p, rope=True, mult=qmul)
                                           for p in range(n_pairs)]),
                                rope=rope, out_dtype=BF16, name="mla_q")
                kw = B_NOPE + B_V
                kn_cols = [hd * kw + t for hd in range(B_HEADS) for t in range(B_NOPE)]
                v_cols = [hd * kw + B_NOPE + t for hd in range(B_HEADS) for t in range(B_V)]
                w_kv = _gather_cols(b_w_kvb[j], kn_cols + v_cols)
                kblocks = [blk for p in range(n_pairs) for blk in (Blk(p), Blk("copy"))]
                kblocks += [Blk(n_pairs + p) for p in range(n_pairs)]
                kv_all = project(cin, b_kvnorm_g[j], w_kv, lay_proj,
                                 x_block=(B_Q_RANK + LANES) // B_KV_RANK, blocks=kblocks,
                                 copy_src=(cin, nq_blocks), out_dtype=BF16, name="mla_kv")
                srcs, akind, extra, out_scale = (q_all, kv_all), "mla", (), 1.0
                w_out = b_w_out[j]
            elif kind == 2:
                perm, slots = _gqa_layout()
                rope = _rope_tables(seq, lay_proj.tile, C_HEAD_DIM // 4, slots)
                nqk = C_HEADS + C_KV_HEADS
                cols = [hd * C_HEAD_DIM + p for hd in range(nqk) for p in perm]
                cols += list(range(nqk * C_HEAD_DIM, (nqk + C_KV_HEADS) * C_HEAD_DIM))
                w_cat = _gather_cols(c_w_in[j], cols)
                gains = jnp.zeros((8, LANES), F32).at[0].set(c_qnorm_g[j][perm]).at[1].set(
                    c_knorm_g[j][perm])
                qmul = C_HEAD_DIM ** -0.5 * LOG2E
                blocks = ([Blk(hd, gain=0, rope=True, mult=qmul) for hd in range(C_HEADS)]
                          + [Blk(C_HEADS + hd, gain=1, rope=True) for hd in range(C_KV_HEADS)]
                          + [Blk(nqk + hd) for hd in range(C_KV_HEADS)])
                qkv = project(h, g0, w_cat, lay_proj, modt=modt, which=(0, 1), blocks=blocks,
                              head_gain=gains, rope=rope, out_dtype=BF16, name="gqa_in")
                srcs, akind, extra, out_scale = (qkv,), "gqa", (), 1.0
                w_out = c_w_out[j]
            else:
                perm, slots = _diff_layout()
                rope = _rope_tables(seq, lay_proj.tile, D_QK_DIM // 4, slots)
                nqk = 2 * D_HEADS
                cols = [hd * 2 * D_QK_DIM + p for hd in range(nqk) for p in perm]
                cols += list(range(nqk * 2 * D_QK_DIM, (nqk + D_HEADS) * 2 * D_QK_DIM))
                w_cat = _gather_cols(d_w_in[j], cols)
                qmul = D_QK_DIM ** -0.5 * LOG2E
                blocks = ([Blk(hd, rope=True, mult=qmul) for hd in range(D_HEADS)]
                          + [Blk(D_HEADS + hd, rope=True) for hd in range(D_HEADS)]
                          + [Blk(nqk + hd) for hd in range(D_HEADS)])
                qkv = project(h, g0, w_cat, lay_proj, modt=modt, which=(0, 1), blocks=blocks,
                              rope=rope, out_dtype=BF16, name="diff_in")
                lambda_init = 0.8 - 0.6 * math.exp(-0.3 * i)
                lam = d_lambda[j].astype(F32)
                lam_full = (jnp.exp(jnp.sum(lam[0] * lam[1])) - jnp.exp(jnp.sum(lam[2] * lam[3]))
                            + lambda_init)
                extra = (jnp.full((1, LANES), lam_full, F32), d_onorm_g[j])
                srcs, akind, out_scale = (qkv,), "diff", 1.0 - lambda_init
                w_out = d_w_out[j]
            a_lat = attention(akind, srcs, batch=batch, seq=seq, ctx_len=ctx_len, queries="lat",
                              extra=extra, out_scale=out_scale)
            a_ctx = None if last else attention(akind, srcs, batch=batch, seq=seq, ctx_len=ctx_len,
                                                queries="ctx", extra=extra, out_scale=out_scale)
            h = out_project((a_lat, a_ctx), w_out.astype(BF16), h, g1, modt, lay_out,
                            name=f"{akind}_out")

        if i % 2 == 0:
            h = ffn_residual(h, g2, g3, ffn_w_gu[i // 2].astype(BF16),
                             ffn_w_down[i // 2].astype(BF16), modt, lay_out)
        else:
            h = moe_residual(h, g2, g3, moe_router[i // 2], moe_w_gu, moe_w_down, i // 2, modt,
                             lay_out)
    return h[:n_lat].reshape(batch, seq, d)
```

```python
import functools
import math
from typing import Any, NamedTuple

import jax
import jax.numpy as jnp
import numpy as np
from jax import lax
from jax.experimental import pallas as pl
from jax.experimental.pallas import tpu as pltpu

F32 = jnp.float32
BF16 = jnp.bfloat16

N_MIXERS = 4
N_MOD = 6
GRID_W = 64
ROPE_THETA = 10000.0
NORM_EPS = 1e-6
A_HEADS = 8
A_DK = 128
A_CHUNK = 32
B_HEADS = 16
B_Q_RANK = 384
B_KV_RANK = 256
B_NOPE = 64
B_ROPE = 32
B_V = 64
C_HEADS = 8
C_KV_HEADS = 2
C_HEAD_DIM = 128
D_HEADS = 8
D_QK_DIM = 64
N_EXPERTS = 8
TOP_K = 2

LANES = 128
V7X_VMEM_BYTES = 64 * 1024 * 1024
VMEM_LIMIT = V7X_VMEM_BYTES - 8 * 1024 * 1024

ROW_TILE = 1024
PROJ_TILE = 512
Q_TILE = {"gqa": 512, "mla": 1024, "diff": 1024}
KEY_BLOCK = 512
LOG2E = 1.4426950408889634
GLA_TILE = 256
GLA_HEADS_PER_STEP = 2
FF_CHUNK = 512
MOE_TILE = 1024


def _cparams(*sem):
    return pltpu.CompilerParams(dimension_semantics=sem, vmem_limit_bytes=VMEM_LIMIT)


def _rms(x, g):
    return x * lax.rsqrt(jnp.mean(x * x, axis=-1, keepdims=True) + NORM_EPS) * g


def _silu(x):
    return x * jax.nn.sigmoid(x)


class Layout:
    def __init__(self, batch, seq, ctx_len, tile_cap, with_ctx=True):
        tile = math.gcd(math.gcd(seq, batch * ctx_len), tile_cap)
        self.batch, self.seq, self.ctx_len, self.tile = batch, seq, ctx_len, tile
        self.n_lat = batch * seq
        self.n_ctx = batch * ctx_len if with_ctx else 0
        self.rows = self.n_lat + self.n_ctx
        assert seq % tile == 0 and self.n_ctx % tile == 0
        self.lat_tiles = self.n_lat // tile
        self.tiles = self.rows // tile
        self.tiles_per_batch = seq // tile

    def mod_row(self, i):
        return jnp.where(i < self.lat_tiles, 1 + i // self.tiles_per_batch, 0)

    def rope_block(self, i):
        return jnp.where(i < self.lat_tiles, i % self.tiles_per_batch, self.tiles_per_batch)

    def mod_spec(self, which, d):
        return pl.BlockSpec((1, 1, d), lambda i, *_: (self.mod_row(i) * N_MOD + which, 0, 0))


def _modvec_kernel(c_ref, w_ref, b_ref, o_ref):
    cond = _silu(c_ref[...]).astype(BF16)
    o_ref[0] = jnp.dot(cond, w_ref[0].astype(BF16), preferred_element_type=F32) + b_ref[0]


def modvec(cond_rows, mod_w, mod_b):
    depth, d, n = mod_w.shape
    r = cond_rows.shape[0]
    tn = d
    return pl.pallas_call(
        _modvec_kernel,
        out_shape=jax.ShapeDtypeStruct((depth, r, n), F32),
        grid=(depth, n // tn),
        in_specs=[pl.BlockSpec((r, d), lambda l, j: (0, 0)),
                  pl.BlockSpec((1, d, tn), lambda l, j: (l, 0, j)),
                  pl.BlockSpec((1, 1, tn), lambda l, j: (l, 0, j))],
        out_specs=pl.BlockSpec((1, r, tn), lambda l, j: (l, 0, j)),
        compiler_params=_cparams("parallel", "parallel"),
        name="modvec",
    )(cond_rows, mod_w, mod_b.reshape(depth, 1, n))


class Blk(NamedTuple):
    src: Any
    gain: Any = None
    rope: bool = False
    mult: Any = None


def _proj_kernel(*refs, has_mod, blocks, vt_srcs, has_gain, has_rope, has_copy):
    it = iter(refs)
    x_ref, g_ref = next(it), next(it)
    sh_ref = sc_ref = hg_ref = c_ref = s_ref = cp_ref = vt_ref = None
    if has_mod:
        sh_ref, sc_ref = next(it), next(it)
    w_ref = next(it)
    if has_gain:
        hg_ref = next(it)
    if has_rope:
        c_ref, s_ref = next(it), next(it)
    if has_copy:
        cp_ref = next(it)
    o_ref = next(it)
    if vt_srcs:
        vt_ref = next(it)
    u_ref = next(it)

    @pl.when(pl.program_id(1) == 0)
    def _():
        y = _rms(x_ref[...].astype(F32), g_ref[...])
        if has_mod:
            y = y * (1 + sc_ref[0]) + sh_ref[0]
        u_ref[...] = y.astype(BF16)

    acc = jnp.dot(u_ref[...], w_ref[...], preferred_element_type=F32)
    if blocks is None:
        o_ref[...] = acc.astype(o_ref.dtype)
        return
    for c, blk in enumerate(blocks):
        if blk.src == "copy":
            a = cp_ref[...].astype(F32)
        else:
            a = acc[:, blk.src * LANES:(blk.src + 1) * LANES]
        if blk.gain is not None:
            a = _rms(a, hg_ref[blk.gain:blk.gain + 1, :])
        if blk.rope:
            a = a * c_ref[...] + pltpu.roll(a, LANES // 2, 1) * s_ref[...]
        if blk.mult is not None:
            a = a * blk.mult
        o_ref[:, c * LANES:(c + 1) * LANES] = a.astype(o_ref.dtype)
    for c, src in enumerate(vt_srcs):
        a = acc[:, src * LANES:(src + 1) * LANES]
        vt_ref[c * LANES:(c + 1) * LANES, :] = a.T.astype(vt_ref.dtype)


def project(x, g, w, lay, *, modt=None, which=None, x_block=0, tn=None, blocks=None, vt_srcs=(),
            head_gain=None, rope=None, copy_src=None, out_dtype=F32, name="project"):
    k, n = w.shape
    tm = lay.tile
    tn = n if tn is None else tn
    assert blocks is None or tn == n
    n_out = n if blocks is None else len(blocks) * LANES
    in_specs = [pl.BlockSpec((tm, k), lambda i, j: (i, x_block)),
                pl.BlockSpec((1, k), lambda i, j: (0, 0))]
    args = [x, g.reshape(1, k)]
    if modt is not None:
        in_specs += [lay.mod_spec(which[0], k), lay.mod_spec(which[1], k)]
        args += [modt, modt]
    in_specs.append(pl.BlockSpec((k, tn), lambda i, j: (0, j)))
    args.append(w)
    if head_gain is not None:
        in_specs.append(pl.BlockSpec(head_gain.shape, lambda i, j: (0, 0)))
        args.append(head_gain)
    if rope is not None:
        spec = pl.BlockSpec((tm, LANES), lambda i, j: (lay.rope_block(i), 0))
        in_specs += [spec, spec]
        args += list(rope)
    if copy_src is not None:
        cp_arr, cp_block = copy_src
        in_specs.append(pl.BlockSpec((tm, LANES), lambda i, j: (i, cp_block)))
        args.append(cp_arr)
    out_shape = [jax.ShapeDtypeStruct((lay.rows, n_out), out_dtype)]
    out_specs = [pl.BlockSpec((tm, n_out if blocks is not None else tn), lambda i, j: (i, j))]
    if vt_srcs:
        out_shape.append(jax.ShapeDtypeStruct((len(vt_srcs) * LANES, lay.rows), BF16))
        out_specs.append(pl.BlockSpec((len(vt_srcs) * LANES, tm), lambda i, j: (0, i)))
    kern = functools.partial(_proj_kernel, has_mod=modt is not None, blocks=blocks,
                             vt_srcs=tuple(vt_srcs), has_gain=head_gain is not None,
                             has_rope=rope is not None, has_copy=copy_src is not None)
    outs = pl.pallas_call(
        kern,
        out_shape=out_shape,
        grid=(lay.tiles, n // tn),
        in_specs=in_specs,
        out_specs=out_specs,
        scratch_shapes=[pltpu.VMEM((tm, k), BF16)],
        compiler_params=_cparams("parallel", "arbitrary"),
        name=name,
    )(*args)
    return outs if vt_srcs else outs[0]


def _oproj_kernel(*refs, lat_tiles, has_ctx, is_gla):
    it = iter(refs)
    a = next(it)[...]
    if has_ctx:
        a_ctx = next(it)[...]
        a = jnp.where(pl.program_id(0) < lat_tiles, a, a_ctx)
    if is_gla:
        gz, og = next(it)[...], next(it)[...]
        parts = []
        for hd in range(A_HEADS):
            sl = slice(hd * LANES, (hd + 1) * LANES)
            parts.append((_rms(a[:, sl], og) * _silu(gz[:, sl])).astype(BF16))
        a = jnp.concatenate(parts, axis=1)
    w_ref, h_ref, g_ref, gate_ref, o_ref = it
    y = jnp.dot(a, w_ref[...], preferred_element_type=F32)
    o_ref[...] = h_ref[...] + gate_ref[0] * _rms(y, g_ref[...])


def out_project(a, w, h, g, modt, lay, *, gla_gate=None, name="out_project"):
    k, d = w.shape
    tm = lay.tile
    row = lambda i: (i, 0)
    a_lat, a_ctx = a
    has_ctx = a_ctx is not None and lay.n_ctx > 0
    last_lat = lay.lat_tiles - 1
    in_specs = [pl.BlockSpec((tm, k), lambda i: (jnp.minimum(i, last_lat), 0))]
    args = [a_lat]
    if has_ctx:
        in_specs.append(pl.BlockSpec((tm, k), lambda i: (jnp.maximum(i - lay.lat_tiles, 0), 0)))
        args.append(a_ctx)
    if gla_gate is not None:
        z, onorm_g, gate_block = gla_gate
        in_specs += [pl.BlockSpec((tm, k), lambda i: (i, gate_block)),
                     pl.BlockSpec((1, LANES), lambda i: (0, 0))]
        args += [z, onorm_g.reshape(1, LANES)]
    in_specs += [pl.BlockSpec((k, d), lambda i: (0, 0)),
                 pl.BlockSpec((tm, d), row),
                 pl.BlockSpec((1, d), lambda i: (0, 0)),
                 lay.mod_spec(2, d)]
    args += [w, h, g.reshape(1, d), modt]
    kern = functools.partial(_oproj_kernel, lat_tiles=lay.lat_tiles, has_ctx=has_ctx,
                             is_gla=gla_gate is not None)
    return pl.pallas_call(
        kern,
        out_shape=jax.ShapeDtypeStruct((lay.rows, d), F32),
        grid=(lay.tiles,),
        in_specs=in_specs,
        out_specs=pl.BlockSpec((tm, d), row),
        compiler_params=_cparams("parallel"),
        name=name,
    )(*args)


def _ffn_kernel(h_ref, g2_ref, sh_ref, sc_ref, wg_ref, wu_ref, wd_ref, g3_ref, gate_ref, o_ref,
                u_ref, acc_ref):
    j = pl.program_id(1)

    @pl.when(j == 0)
    def _():
        y = _rms(h_ref[...], g2_ref[...]) * (1 + sc_ref[0]) + sh_ref[0]
        u_ref[...] = y.astype(BF16)
        acc_ref[...] = jnp.zeros_like(acc_ref)

    u = u_ref[...]
    a = jnp.dot(u, wg_ref[...], preferred_element_type=F32)
    b = jnp.dot(u, wu_ref[...], preferred_element_type=F32)
    hid = (_silu(a) * b).astype(BF16)
    acc_ref[...] += jnp.dot(hid, wd_ref[...], preferred_element_type=F32)

    @pl.when(j == pl.num_programs(1) - 1)
    def _():
        o_ref[...] = h_ref[...] + gate_ref[0] * _rms(acc_ref[...], g3_ref[...])


def ffn_residual(h, g2, g3, w_gu, w_down, modt, lay):
    d = h.shape[1]
    ff = w_down.shape[0]
    fc = FF_CHUNK
    nfc = ff // fc
    tm = lay.tile
    row = lambda i, j: (i, 0)
    vec = pl.BlockSpec((1, d), lambda i, j: (0, 0))
    return pl.pallas_call(
        _ffn_kernel,
        out_shape=jax.ShapeDtypeStruct((lay.rows, d), F32),
        grid=(lay.tiles, nfc),
        in_specs=[pl.BlockSpec((tm, d), row), vec, lay.mod_spec(3, d), lay.mod_spec(4, d),
                  pl.BlockSpec((d, fc), lambda i, j: (0, j)),
                  pl.BlockSpec((d, fc), lambda i, j: (0, j + nfc)),
                  pl.BlockSpec((fc, d), lambda i, j: (j, 0)),
                  vec, lay.mod_spec(5, d)],
        out_specs=pl.BlockSpec((tm, d), row),
        scratch_shapes=[pltpu.VMEM((tm, d), BF16), pltpu.VMEM((tm, d), F32)],
        compiler_params=_cparams("parallel", "arbitrary"),
        name="ffn_residual",
    )(h, g2.reshape(1, d), modt, modt, w_gu, w_gu, w_down, g3.reshape(1, d), modt)


def _router_kernel(h_ref, g2_ref, sh_ref, sc_ref, r_ref, u_ref, ids_ref, wts_ref):
    u = (_rms(h_ref[...], g2_ref[...]) * (1 + sc_ref[0]) + sh_ref[0]).astype(BF16)
    u_ref[...] = u
    logits = jnp.dot(u, r_ref[...], preferred_element_type=F32)
    lane = lax.broadcasted_iota(jnp.int32, logits.shape, 1)
    neg = jnp.float32(-jnp.inf)
    logits = jnp.where(lane < N_EXPERTS, logits, neg)
    m1 = jnp.max(logits, axis=-1, keepdims=True)
    i1 = jnp.min(jnp.where(logits == m1, lane, LANES), axis=-1, keepdims=True)
    rest = jnp.where(lane == i1, neg, logits)
    m2 = jnp.max(rest, axis=-1, keepdims=True)
    i2 = jnp.min(jnp.where(rest == m2, lane, LANES), axis=-1, keepdims=True)
    e = jnp.exp(m2 - m1)
    w1 = 1.0 / (1.0 + e)
    w2 = e / (1.0 + e)
    ids_ref[...] = jnp.where(lane == 0, i1, jnp.where(lane == 1, i2, 0))
    wts_ref[...] = jnp.where(lane == 0, w1, jnp.where(lane == 1, w2, 0.0))


def moe_router(h, g2, router, modt, lay):
    d = h.shape[1]
    tm = lay.tile
    r_pad = jnp.zeros((d, LANES), BF16).at[:, :N_EXPERTS].set(router.astype(BF16))
    row = lambda i: (i, 0)
    return pl.pallas_call(
        _router_kernel,
        out_shape=(jax.ShapeDtypeStruct((lay.rows, d), BF16),
                   jax.ShapeDtypeStruct((lay.rows, LANES), jnp.int32),
                   jax.ShapeDtypeStruct((lay.rows, LANES), F32)),
        grid=(lay.tiles,),
        in_specs=[pl.BlockSpec((tm, d), row), pl.BlockSpec((1, d), lambda i: (0, 0)),
                  lay.mod_spec(3, d), lay.mod_spec(4, d),
                  pl.BlockSpec((d, LANES), lambda i: (0, 0))],
        out_specs=(pl.BlockSpec((tm, d), row), pl.BlockSpec((tm, LANES), row),
                   pl.BlockSpec((tm, LANES), row)),
        compiler_params=_cparams("parallel"),
        name="moe_router",
    )(h, g2.reshape(1, d), modt, modt, r_pad)


def _moe_ffn_kernel(te_ref, nu_ref, x_ref, wg_ref, wu_ref, wd_ref, o_ref, acc_ref):
    t, j = pl.program_id(0), pl.program_id(1)

    @pl.when(j == 0)
    def _():
        acc_ref[...] = jnp.zeros_like(acc_ref)

    @pl.when(t < nu_ref[0])
    def _():
        x = x_ref[...]
        a = jnp.dot(x, wg_ref[0, 0].astype(BF16), preferred_element_type=F32)
        b = jnp.dot(x, wu_ref[0, 0].astype(BF16), preferred_element_type=F32)
        hid = (_silu(a) * b).astype(BF16)
        acc_ref[...] += jnp.dot(hid, wd_ref[0, 0].astype(BF16), preferred_element_type=F32)

    @pl.when(j == pl.num_programs(1) - 1)
    def _():
        o_ref[...] = acc_ref[...].astype(o_ref.dtype)


def moe_ffn(x_sorted, tile_expert, n_used, w_gu, w_down, layer):
    n_slots, d = x_sorted.shape
    ff = w_down.shape[2]
    fc = FF_CHUNK
    nfc = ff // fc
    tm = MOE_TILE
    n_tiles = n_slots // tm

    def jj(t, j, te, nu):
        return jnp.where(t < nu[0], j, nfc - 1)

    grid_spec = pltpu.PrefetchScalarGridSpec(
        num_scalar_prefetch=2,
        grid=(n_tiles, nfc),
        in_specs=[pl.BlockSpec((tm, d), lambda t, j, te, nu: (t, 0)),
                  pl.BlockSpec((1, 1, d, fc),
                               lambda t, j, te, nu: (layer, te[t], 0, jj(t, j, te, nu))),
                  pl.BlockSpec((1, 1, d, fc),
                               lambda t, j, te, nu: (layer, te[t], 0, jj(t, j, te, nu) + nfc)),
                  pl.BlockSpec((1, 1, fc, d),
                               lambda t, j, te, nu: (layer, te[t], jj(t, j, te, nu), 0))],
        out_specs=pl.BlockSpec((tm, d), lambda t, j, te, nu: (t, 0)),
        scratch_shapes=[pltpu.VMEM((tm, d), F32)],
    )
    return pl.pallas_call(
        _moe_ffn_kernel,
        out_shape=jax.ShapeDtypeStruct((n_slots, d), BF16),
        grid_spec=grid_spec,
        compiler_params=_cparams("arbitrary", "arbitrary"),
        name="moe_ffn",
    )(tile_expert, n_used, x_sorted, w_gu, w_gu, w_down)


def _combine_kernel(y_ref, w_ref, h_ref, g_ref, gate_ref, o_ref):
    w = w_ref[...]
    y = w[:, 0:1] * y_ref[0].astype(F32) + w[:, 1:2] * y_ref[1].astype(F32)
    o_ref[...] = h_ref[...] + gate_ref[0] * _rms(y, g_ref[...])


def combine_residual(y_pair, wts, h, g3, modt, lay):
    d = h.shape[1]
    tm = lay.tile
    row = lambda i: (i, 0)
    return pl.pallas_call(
        _combine_kernel,
        out_shape=jax.ShapeDtypeStruct((lay.rows, d), F32),
        grid=(lay.tiles,),
        in_specs=[pl.BlockSpec((2, tm, d), lambda i: (0, i, 0)), pl.BlockSpec((tm, LANES), row),
                  pl.BlockSpec((tm, d), row), pl.BlockSpec((1, d), lambda i: (0, 0)),
                  lay.mod_spec(5, d)],
        out_specs=pl.BlockSpec((tm, d), row),
        compiler_params=_cparams("parallel"),
        name="moe_combine",
    )(y_pair, wts, h, g3.reshape(1, d), modt)


def moe_residual(h, g2, g3, router, w_gu, w_down, layer, modt, lay):
    rows = lay.rows
    u, ids, wts = moe_router(h, g2, router, modt, lay)
    e_flat = ids[:, :TOP_K].reshape(-1)
    n = rows * TOP_K
    onehot = (e_flat[:, None] == jnp.arange(N_EXPERTS, dtype=jnp.int32)[None, :]).astype(jnp.int32)
    csum = jnp.cumsum(onehot, axis=0)
    counts = csum[-1]
    rank = jnp.sum(csum * onehot, axis=1) - 1
    padded = ((counts + MOE_TILE - 1) // MOE_TILE) * MOE_TILE
    ends = jnp.cumsum(padded)
    starts = ends - padded
    dest = jnp.sum(starts[None, :] * onehot, axis=1) + rank
    n_slots = ((n + N_EXPERTS * (MOE_TILE - 1)) // MOE_TILE + 1) * MOE_TILE
    n_tiles = n_slots // MOE_TILE
    src_tok = jnp.zeros((n_slots,), jnp.int32).at[dest].set(
        jnp.arange(n, dtype=jnp.int32) // TOP_K, unique_indices=True)
    n_used = (ends[-1] // MOE_TILE).astype(jnp.int32).reshape(1)
    tile_start = jnp.arange(n_tiles, dtype=jnp.int32) * MOE_TILE
    tile_expert = jnp.minimum(
        jnp.sum((ends[None, :] <= tile_start[:, None]).astype(jnp.int32), axis=1), N_EXPERTS - 1)
    last_e = jnp.max(jnp.where(counts > 0, jnp.arange(N_EXPERTS, dtype=jnp.int32), 0))
    tile_expert = jnp.where(tile_start < ends[-1], tile_expert, last_e)
    x_sorted = u.at[src_tok].get(mode="promise_in_bounds")
    y_sorted = moe_ffn(x_sorted, tile_expert, n_used, w_gu, w_down, layer)
    y_pair = y_sorted.at[dest.reshape(rows, TOP_K).T].get(mode="promise_in_bounds")
    return combine_residual(y_pair, wts, h, g3, modt, lay)


def _gla_kernel(q_ref, ff_ref, fb_ref, v_ref, lbf_ref, lbb_ref, s0_ref, o_ref, st_ref,
                s_scr, ob_scr, *, n_tiles):
    rt = GLA_TILE
    nch = rt // A_CHUNK
    s_scr[...] = s0_ref[0]
    row = lax.broadcasted_iota(jnp.int32, (rt, LANES), 0) % A_CHUNK
    ci = lax.broadcasted_iota(jnp.int32, (rt, rt), 0)
    si = lax.broadcasted_iota(jnp.int32, (rt, rt), 1)
    same_chunk = (ci // A_CHUNK) == (si // A_CHUNK)
    nt = (((1,), (1,)), ((), ()))

    def direction(r0, f_ref, lb_ref, hd, d, reverse):
        rows = pl.ds(pl.multiple_of(r0, rt), rt)
        cols = slice(hd * LANES, (hd + 1) * LANES)
        lb = lb_ref[:, cols]
        f = lb + (1.0 - lb) * jax.nn.sigmoid(f_ref[rows, cols])
        logf = jnp.log(f)
        kk = 1.0 - f
        q = q_ref[rows, cols] * (A_DK ** -0.5)
        v = v_ref[rows, cols]
        cum = logf
        for s in (1, 2, 4, 8, 16):
            if reverse:
                cum = cum + jnp.where(row < A_CHUNK - s, pltpu.roll(cum, rt - s, 0), 0.0)
            else:
                cum = cum + jnp.where(row >= s, pltpu.roll(cum, s, 0), 0.0)
        cum3 = cum.reshape(nch, A_CHUNK, LANES)
        end3 = cum3[:, 0:1, :] if reverse else cum3[:, A_CHUNK - 1:A_CHUNK, :]
        q_dec = (q * jnp.exp(cum)).astype(BF16)
        k_neg = (kk * jnp.exp(-cum)).astype(BF16)
        k_end = (kk.reshape(nch, A_CHUNK, LANES) * jnp.exp(end3 - cum3)).astype(BF16)
        vb = v.astype(BF16)
        decay = jnp.exp(end3)
        att = lax.dot_general(q_dec, k_neg, nt, preferred_element_type=F32)
        keep = same_chunk & ((si >= ci) if reverse else (si <= ci))
        o_intra = jnp.dot(jnp.where(keep, att, 0.0).astype(BF16), vb, preferred_element_type=F32)
        q_dec3 = q_dec.reshape(nch, A_CHUNK, LANES)
        v3 = vb.reshape(nch, A_CHUNK, LANES)
        upds = [lax.dot_general(v3[g], k_end[g], (((0,), (0,)), ((), ())),
                                preferred_element_type=F32) for g in range(nch)]
        before = [None] * nch
        state = s_scr[hd, d]
        for g in (range(nch - 1, -1, -1) if reverse else range(nch)):
            before[g] = state.astype(BF16)
            state = decay[g] * state + upds[g]
        s_scr[hd, d] = state
        outs = [lax.dot_general(q_dec3[g], before[g], nt, preferred_element_type=F32)
                for g in range(nch)]
        return o_intra + jnp.concatenate(outs, axis=0), rows, cols

    def body(t, carry):
        for hd in range(GLA_HEADS_PER_STEP):
            o_f, rows_f, cols = direction(t * rt, ff_ref, lbf_ref, hd, 0, False)
            o_ref[rows_f, cols] = o_f
            o_b, rows_b, cols = direction((n_tiles - 1 - t) * rt, fb_ref, lbb_ref, hd, 1, True)
            ob_scr[rows_b, cols] = o_b
        return carry

    lax.fori_loop(0, n_tiles, body, 0)
    o_ref[...] += ob_scr[...]
    st_ref[0] = s_scr[...]


def gla_scan(z, lb_f, lb_b, s0, *, batch, seg_len, row0):
    n_tiles = seg_len // GLA_TILE
    rb0 = row0 // seg_len
    assert row0 % seg_len == 0

    hps = GLA_HEADS_PER_STEP
    gw = hps * LANES

    def col(kind):
        return pl.BlockSpec((seg_len, gw), lambda b, h: (rb0 + b, kind * (A_HEADS // hps) + h))

    lbspec = pl.BlockSpec((1, gw), lambda b, h: (0, h))
    sspec = pl.BlockSpec((1, hps, 2, LANES, LANES), lambda b, h: (b, h, 0, 0, 0))
    kern = functools.partial(_gla_kernel, n_tiles=n_tiles)
    return pl.pallas_call(
        kern,
        out_shape=(jax.ShapeDtypeStruct((batch * seg_len, A_HEADS * LANES), F32),
                   jax.ShapeDtypeStruct(s0.shape, F32)),
        grid=(batch, A_HEADS // hps),
        in_specs=[col(0), col(1), col(2), col(3), lbspec, lbspec, sspec],
        out_specs=(pl.BlockSpec((seg_len, gw), lambda b, h: (b, h)), sspec),
        scratch_shapes=[pltpu.VMEM((hps, 2, LANES, LANES), F32), pltpu.VMEM((seg_len, gw), F32)],
        compiler_params=_cparams("parallel", "parallel"),
        name="gla_scan",
    )(z, z, z, z, lb_f.reshape(1, -1), lb_b.reshape(1, -1), s0)


def _flash(q, kc, vc, kl_ref, vl_ref):
    nt = (((1,), (1,)), ((), ()))
    n_kblocks = 0 if kl_ref is None else kl_ref.shape[0] // KEY_BLOCK
    m = l = acc = None
    for blk in range(n_kblocks + 1):
        if blk == 0:
            kb, vb = kc, vc
        else:
            ks = slice((blk - 1) * KEY_BLOCK, blk * KEY_BLOCK)
            kb, vb = kl_ref[ks, :], vl_ref[ks, :]
        s = lax.dot_general(q, kb, nt, preferred_element_type=F32)
        m_blk = jnp.max(s, axis=-1, keepdims=True)
        m_new = m_blk if blk == 0 else jnp.maximum(m, m_blk)
        p = jnp.exp2(s - m_new)
        psum = p[:, :LANES]
        for t in range(1, p.shape[1] // LANES):
            psum = psum + p[:, t * LANES:(t + 1) * LANES]
        pv = jnp.dot(p.astype(BF16), vb, preferred_element_type=F32)
        if blk == 0:
            l, acc = psum, pv
        else:
            alpha = jnp.exp2(m - m_new)
            l = alpha * l + psum
            acc = alpha * acc + pv
        m = m_new
    return acc / jnp.sum(l, axis=-1, keepdims=True)


def _split_attn_refs(refs, n_lead, has_lat):
    lead = refs[:n_lead]
    kc_ref, vc_ref = refs[n_lead], refs[n_lead + 1]
    if has_lat:
        kl_ref, vl_ref, o_ref = refs[n_lead + 2:]
    else:
        kl_ref = vl_ref = None
        (o_ref,) = refs[n_lead + 2:]
    return lead, kc_ref, vc_ref, kl_ref, vl_ref, o_ref


def _attn_gqa_kernel(*refs, has_lat):
    (q_ref,), kc_ref, vc_ref, kl_ref, vl_ref, o_ref = _split_attn_refs(refs, 1, has_lat)
    group = C_HEADS // C_KV_HEADS
    tq = q_ref.shape[0]
    q = jnp.concatenate([q_ref[:, g * LANES:(g + 1) * LANES] for g in range(group)], axis=0)
    o = _flash(q, kc_ref[...], vc_ref[...], kl_ref, vl_ref)
    for g in range(group):
        o_ref[:, g * LANES:(g + 1) * LANES] = o[g * tq:(g + 1) * tq].astype(o_ref.dtype)


def _attn_mla_kernel(*refs, has_lat):
    (qn_ref, qr_ref), kc_ref, vc_ref, kl_ref, vl_ref, o_ref = _split_attn_refs(refs, 2, has_lat)
    qn, qr = qn_ref[...], qr_ref[...]
    tq = qn.shape[0]
    lane = lax.broadcasted_iota(jnp.int32, qn.shape, 1)
    zero = jnp.zeros_like(qn)
    qs = []
    for hd in range(2):
        mn = (lane // B_NOPE) == hd
        mr = ((lane % (LANES // 2)) // (B_ROPE // 2)) == hd
        qs.append(jnp.concatenate([jnp.where(mn, qn, zero), jnp.where(mr, qr, zero)], axis=1))
    o = _flash(jnp.concatenate(qs, axis=0), kc_ref[...], vc_ref[...], kl_ref, vl_ref)
    o_ref[...] = jnp.where(lane < B_V, o[:tq], o[tq:]).astype(o_ref.dtype)


def _attn_diff_kernel(*refs, has_lat, out_scale):
    (q_ref, lam_ref, og_ref), kc_ref, vc_ref, kl_ref, vl_ref, o_ref = _split_attn_refs(
        refs, 3, has_lat)
    q = q_ref[...]
    tq = q.shape[0]
    lane = lax.broadcasted_iota(jnp.int32, q.shape, 1)
    zero = jnp.zeros_like(q)
    qs = [jnp.where(((lane // (D_QK_DIM // 2)) % 2) == comp, q, zero) for comp in range(2)]
    maps = _flash(jnp.concatenate(qs, axis=0), kc_ref[...], vc_ref[...], kl_ref, vl_ref)
    o = maps[:tq] - lam_ref[...] * maps[tq:]
    o_ref[...] = (_rms(o, og_ref[...]) * out_scale).astype(o_ref.dtype)


def attention(kind, srcs, *, batch, seq, ctx_len, queries, extra=(), out_scale=1.0):
    n_lat = batch * seq
    lat = queries == "lat"
    tq = min(Q_TILE[kind], seq if lat else ctx_len)
    nq = (seq if lat else ctx_len) // tq
    q0 = 0 if lat else n_lat // tq
    c0 = n_lat // ctx_len
    rows = batch * (seq if lat else ctx_len)

    def qspec(width, colfn):
        return pl.BlockSpec((tq, width), lambda b, g, i: (q0 + b * nq + i, colfn(g)))

    def kvspecs(kwidth, kcol, vcol):
        specs = [pl.BlockSpec((ctx_len, kwidth), lambda b, g, i: (c0 + b, kcol(g))),
                 pl.BlockSpec((ctx_len, LANES), lambda b, g, i: (c0 + b, vcol(g)))]
        if lat:
            specs += [pl.BlockSpec((seq, kwidth), lambda b, g, i: (b, kcol(g))),
                      pl.BlockSpec((seq, LANES), lambda b, g, i: (b, vcol(g)))]
        return specs

    n_kv = 4 if lat else 2
    if kind == "gqa":
        (qkv,) = srcs
        groups = C_KV_HEADS
        ow = (C_HEADS // C_KV_HEADS) * LANES
        in_specs = [qspec(ow, lambda g: g)] + kvspecs(
            LANES, lambda g: C_HEADS + g, lambda g: C_HEADS + C_KV_HEADS + g)
        args = [qkv] * (1 + n_kv)
        kern = functools.partial(_attn_gqa_kernel, has_lat=lat)
    elif kind == "mla":
        q_all, kv_all = srcs
        groups = B_HEADS // 2
        ow = LANES
        in_specs = ([qspec(LANES, lambda g: g), qspec(LANES, lambda g: groups + g)]
                    + kvspecs(2 * LANES, lambda g: g, lambda g: 2 * groups + g))
        args = [q_all, q_all] + [kv_all] * n_kv
        kern = functools.partial(_attn_mla_kernel, has_lat=lat)
    else:
        (qkv,) = srcs
        lam, onorm_g = extra
        groups = D_HEADS
        ow = LANES
        vec = pl.BlockSpec((1, LANES), lambda b, g, i: (0, 0))
        in_specs = [qspec(LANES, lambda g: g), vec, vec] + kvspecs(
            LANES, lambda g: D_HEADS + g, lambda g: 2 * D_HEADS + g)
        args = [qkv, lam, onorm_g.reshape(1, LANES)] + [qkv] * n_kv
        kern = functools.partial(_attn_diff_kernel, has_lat=lat, out_scale=out_scale)
    return pl.pallas_call(
        kern,
        out_shape=jax.ShapeDtypeStruct((rows, groups * ow), BF16),
        grid=(batch, groups, nq),
        in_specs=in_specs,
        out_specs=pl.BlockSpec((tq, ow), lambda b, g, i: (b * nq + i, g)),
        compiler_params=_cparams("parallel", "parallel", "parallel"),
        name=f"attn_{kind}_{queries}",
    )(*args)


def _rope_tables(seq, tile, n_pairs, lane_slots):
    axis = np.asarray([0 if s is None else s[0] for s in lane_slots])
    freq = np.asarray([0 if s is None else s[1] for s in lane_slots])
    sign = np.asarray([0.0 if s is None else s[2] for s in lane_slots], np.float32)
    live = np.asarray([s is not None for s in lane_slots])
    t = jnp.arange(seq, dtype=jnp.int32)
    pos = jnp.where(jnp.asarray(axis)[None, :] == 0, (t // GRID_W)[:, None], (t % GRID_W)[:, None])
    inv = jnp.asarray(ROPE_THETA, F32) ** (-jnp.arange(0, 2 * n_pairs, 2, dtype=F32) / (2 * n_pairs))
    ang = pos.astype(F32) * inv[jnp.asarray(freq)][None, :]
    cos = jnp.where(jnp.asarray(live)[None, :], jnp.cos(ang), 1.0)
    sin = jnp.asarray(sign)[None, :] * jnp.sin(ang)
    ctab = jnp.concatenate([cos, jnp.ones((tile, LANES), F32)], axis=0)
    stab = jnp.concatenate([sin, jnp.zeros((tile, LANES), F32)], axis=0)
    return ctab, stab


def _gqa_layout():
    q = C_HEAD_DIM // 4
    perm = np.concatenate([np.arange(0, q), np.arange(2 * q, 3 * q), np.arange(q, 2 * q),
                           np.arange(3 * q, 4 * q)])
    slots = [(axis, j, sign) for sign in (-1.0, 1.0) for axis in (0, 1) for j in range(q)]
    return perm, slots


def _diff_layout():
    e = D_QK_DIM // 4
    perm, slots = [], []
    for half, sign in ((0, -1.0), (1, 1.0)):
        for comp in range(2):
            for axis in range(2):
                perm.extend(comp * D_QK_DIM + axis * 2 * e + half * e + j for j in range(e))
                slots.extend((axis, j, sign) for j in range(e))
    return np.asarray(perm), slots


def _mla_rope_layout():
    e = B_ROPE // 4
    dims, slots = [], []
    for half, sign in ((0, -1.0), (1, 1.0)):
        for hd in range(2):
            for axis in range(2):
                dims.extend((hd, axis * 2 * e + half * e + j) for j in range(e))
                slots.extend((axis, j, sign) for j in range(e))
        dims.extend([None] * (2 * B_ROPE // 2))
        slots.extend([None] * (2 * B_ROPE // 2))
    return dims, slots


def _gather_cols(w, cols):
    idx = np.asarray([0 if c is None else c for c in cols])
    keep = np.asarray([c is not None for c in cols])
    out = jnp.take(w, jnp.asarray(idx), axis=1)
    return jnp.where(jnp.asarray(keep)[None, :], out, 0).astype(BF16)


def kernel(x, c, ctx, c_ctx, mod_w, mod_b, norm_g, a_w_in, a_lb, a_onorm_g, a_w_out, b_w_in, b_qnorm_g, b_kvnorm_g, b_w_qb, b_w_kvb, b_w_out, c_w_in, c_qnorm_g, c_knorm_g, c_w_out, d_w_in, d_lambda, d_onorm_g, d_w_out, ffn_w_gu, ffn_w_down, moe_router, moe_w_gu, moe_w_down):
    batch, seq, d = x.shape
    ctx_len = ctx.shape[1]
    depth = mod_w.shape[0]
    n_lat = batch * seq
    lay_all = Layout(batch, seq, ctx_len, ROW_TILE, with_ctx=True)
    lay_lat = Layout(batch, seq, ctx_len, ROW_TILE, with_ctx=False)
    lay_proj = Layout(batch, seq, ctx_len, PROJ_TILE, with_ctx=True)

    h = jnp.concatenate([x.reshape(n_lat, d), ctx.reshape(batch * ctx_len, d)], axis=0)

    n_rows = 8 * ((batch + 1 + 7) // 8)
    cond = jnp.zeros((n_rows, d), F32).at[0].set(c_ctx).at[1:batch + 1].set(c)
    mod_all = modvec(cond, mod_w, mod_b)

    lower_bounds = jnp.cumsum(jax.nn.softmax(a_lb.astype(F32), axis=1), axis=1)

    for i in range(depth):
        last = i == depth - 1
        kind, j = i % N_MIXERS, i // N_MIXERS
        modt = mod_all[i].reshape(n_rows * N_MOD, 1, d)
        lay_out = lay_lat if last else lay_all
        g0, g1, g2, g3 = norm_g[i, 0], norm_g[i, 1], norm_g[i, 2], norm_g[i, 3]

        if kind == 0:
            z = project(h, g0, a_w_in[j].astype(BF16), lay_all, modt=modt, which=(0, 1),
                        tn=A_HEADS * LANES, name="gla_in")
            s0 = jnp.zeros((batch, A_HEADS, 2, LANES, LANES), F32)
            o_ctx, s_ctx = gla_scan(z, lower_bounds[0, i], lower_bounds[1, i], s0,
                                    batch=batch, seg_len=ctx_len, row0=n_lat)
            o_lat, _ = gla_scan(z, lower_bounds[0, i], lower_bounds[1, i], s_ctx,
                                batch=batch, seg_len=seq, row0=0)
            h = out_project((o_lat, o_ctx), a_w_out[j].astype(BF16), h, g1, modt, lay_out,
                            gla_gate=(z, a_onorm_g[j], 4), name="gla_out")
        else:
            if kind == 1:
                dims, slots = _mla_rope_layout()
                rope = _rope_tables(seq, lay_proj.tile, B_ROPE // 4, slots)
                w_in = b_w_in[j]
                kr_cols = [None if dd is None else B_Q_RANK + B_KV_RANK + dd[1] for dd in dims]
                w_cat = jnp.concatenate(
                    [w_in[:, :B_Q_RANK].astype(BF16), _gather_cols(w_in, kr_cols),
                     w_in[:, B_Q_RANK:B_Q_RANK + B_KV_RANK].astype(BF16)], axis=1)
                nq_blocks = B_Q_RANK // LANES
                n_in_blocks = nq_blocks + 1 + B_KV_RANK // LANES
                blocks = [Blk(cb, rope=(cb == nq_blocks)) for cb in range(n_in_blocks)]
                cin = project(h, g0, w_cat, lay_proj, modt=modt, which=(0, 1), blocks=blocks,
                              rope=rope, name="mla_in")
                hw = B_NOPE + B_ROPE
                n_pairs = B_HEADS // 2
                qn_cols = [hd * hw + t for hd in range(B_HEADS) for t in range(B_NOPE)]
                qr_cols = [None if dd is None else (2 * p + dd[0]) * hw + B_NOPE + dd[1]
                           for p in range(n_pairs) for dd in dims]
                w_q = _gather_cols(b_w_qb[j], qn_cols + qr_cols)
                qmul = (B_NOPE + B_ROPE) ** -0.5 * LOG2E
                q_all = project(cin, b_qnorm_g[j], w_q, lay_proj, x_block=0,
                                blocks=([Blk(p, mult=qmul) for p in range(n_pairs)]
                                        + [Blk(n_pairs + p, rope=True, mult=qmul)
                                           for p in range(n_pairs)]),
                                rope=rope, out_dtype=BF16, name="mla_q")
                kw = B_NOPE + B_V
                kn_cols = [hd * kw + t for hd in range(B_HEADS) for t in range(B_NOPE)]
                v_cols = [hd * kw + B_NOPE + t for hd in range(B_HEADS) for t in range(B_V)]
                w_kv = _gather_cols(b_w_kvb[j], kn_cols + v_cols)
                kblocks = [blk for p in range(n_pairs) for blk in (Blk(p), Blk("copy"))]
                kblocks += [Blk(n_pairs + p) for p in range(n_pairs)]
                kv_all = project(cin, b_kvnorm_g[j], w_kv, lay_proj,
                                 x_block=(B_Q_RANK + LANES) // B_KV_RANK, blocks=kblocks,
                                 copy_src=(cin, nq_blocks), out_dtype=BF16, name="mla_kv")
                srcs, akind, extra, out_scale = (q_all, kv_all), "mla", (), 1.0
                w_out = b_w_out[j]
            elif kind == 2:
                perm, slots = _gqa_layout()
                rope = _rope_tables(seq, lay_proj.tile, C_HEAD_DIM // 4, slots)
                nqk = C_HEADS + C_KV_HEADS
                cols = [hd * C_HEAD_DIM + p for hd in range(nqk) for p in perm]
                cols += list(range(nqk * C_HEAD_DIM, (nqk + C_KV_HEADS) * C_HEAD_DIM))
                w_cat = _gather_cols(c_w_in[j], cols)
                gains = jnp.zeros((8, LANES), F32).at[0].set(c_qnorm_g[j][perm]).at[1].set(
                    c_knorm_g[j][perm])
                qmul = C_HEAD_DIM ** -0.5 * LOG2E
                blocks = ([Blk(hd, gain=0, rope=True, mult=qmul) for hd in range(C_HEADS)]
                          + [Blk(C_HEADS + hd, gain=1, rope=True) for hd in range(C_KV_HEADS)]
                          + [Blk(nqk + hd) for hd in range(C_KV_HEADS)])
                qkv = project(h, g0, w_cat, lay_proj, modt=modt, which=(0, 1), blocks=blocks,
                              head_gain=gains, rope=rope, out_dtype=BF16, name="gqa_in")
                srcs, akind, extra, out_scale = (qkv,), "gqa", (), 1.0
                w_out = c_w_out[j]
            else:
                perm, slots = _diff_layout()
                rope = _rope_tables(seq, lay_proj.tile, D_QK_DIM // 4, slots)
                nqk = 2 * D_HEADS
                cols = [hd * 2 * D_QK_DIM + p for hd in range(nqk) for p in perm]
                cols += list(range(nqk * 2 * D_QK_DIM, (nqk + D_HEADS) * 2 * D_QK_DIM))
                w_cat = _gather_cols(d_w_in[j], cols)
                qmul = D_QK_DIM ** -0.5 * LOG2E
                blocks = ([Blk(hd, rope=True, mult=qmul) for hd in range(D_HEADS)]
                          + [Blk(D_HEADS + hd, rope=True) for hd in range(D_HEADS)]
                          + [Blk(nqk + hd) for hd in range(D_HEADS)])
                qkv = project(h, g0, w_cat, lay_proj, modt=modt, which=(0, 1), blocks=blocks,
                              rope=rope, out_dtype=BF16, name="diff_in")
                lambda_init = 0.8 - 0.6 * math.exp(-0.3 * i)
                lam = d_lambda[j].astype(F32)
                lam_full = (jnp.exp(jnp.sum(lam[0] * lam[1])) - jnp.exp(jnp.sum(lam[2] * lam[3]))
                            + lambda_init)
                extra = (jnp.full((1, LANES), lam_full, F32), d_onorm_g[j])
                srcs, akind, out_scale = (qkv,), "diff", 1.0 - lambda_init
                w_out = d_w_out[j]
            a_lat = attention(akind, srcs, batch=batch, seq=seq, ctx_len=ctx_len, queries="lat",
                              extra=extra, out_scale=out_scale)
            a_ctx = None if last else attention(akind, srcs, batch=batch, seq=seq, ctx_len=ctx_len,
                                                queries="ctx", extra=extra, out_scale=out_scale)
            h = out_project((a_lat, a_ctx), w_out.astype(BF16), h, g1, modt, lay_out,
                            name=f"{akind}_out")

        if i % 2 == 0:
            h = ffn_residual(h, g2, g3, ffn_w_gu[i // 2].astype(BF16),
                             ffn_w_down[i // 2].astype(BF16), modt, lay_out)
        else:
            h = moe_residual(h, g2, g3, moe_router[i // 2], moe_w_gu, moe_w_down, i // 2, modt,
                             lay_out)
    return h[:n_lat].reshape(batch, seq, d)
```
